```python
import math
import jax
import jax.numpy as jnp
from jax import lax
import numpy as np

D_MODEL = 4096
BATCH = 2
SEQ = 8192
DEPTH = 4

CTX_LEN = 256
GRID_W = 64
NORM_EPS = 1e-6
F32 = jnp.float32

ADA_RANK = 256
N_MOD = 6
FFN_DIM = -(-8 * D_MODEL // (3 * 256)) * 256
MIX_HALF = D_MODEL // 2

HY_DIM = MIX_HALF
HY_ORDER = 2
HY_SHORT = 3
HY_BANDS = 16
HY_EMB = 1 + 2 * HY_BANDS
HY_FFN = 64
HY_TARGET = 1e-2
HY_FAST_PCT = 0.3
HY_SLOW_PCT = 1.5

RW_HEAD = 64
RW_DIM = MIX_HALF
RW_HEADS = RW_DIM // RW_HEAD
RW_W_LORA = 96
RW_A_LORA = 96
RW_G_LORA = 256
RW_DECAY_SCALE = math.exp(-0.5)
RW_GN_EPS = 64e-5
RW_COLS = 3 * RW_DIM + RW_W_LORA + RW_A_LORA + RW_G_LORA
RW_SPLITS = [RW_DIM, 2 * RW_DIM, 3 * RW_DIM, 3 * RW_DIM + RW_W_LORA, 3 * RW_DIM + RW_W_LORA + RW_A_LORA]
EV_COLS = 3 * HY_DIM + RW_COLS

HG_DIM = MIX_HALF
HG_HEAD = 128
HG_HEADS = HG_DIM // HG_HEAD
HG_CHUNK = 64

AT_HEAD = 128
AT_Q_HEADS = MIX_HALF // AT_HEAD
AT_KV_HEADS = 4
AT_GROUP = AT_Q_HEADS // AT_KV_HEADS
AT_BLOCK = 128
ROPE_THETA = 10000.0
ROPE_AXIS = AT_HEAD // 2
OD_COLS = 5 * HG_DIM + (AT_Q_HEADS + 2 * AT_KV_HEADS) * AT_HEAD
OD_SPLITS = [HG_DIM, 2 * HG_DIM, 3 * HG_DIM, 4 * HG_DIM, 5 * HG_DIM,
             5 * HG_DIM + AT_Q_HEADS * AT_HEAD, 5 * HG_DIM + (AT_Q_HEADS + AT_KV_HEADS) * AT_HEAD]

N_EVEN = (DEPTH + 1) // 2
N_ODD = DEPTH // 2

kernel_name = 'hybrid_hyena_rwkv7_hgrn2_gqa_dit'


def rmsnorm(x, g):
    xf = x.astype(F32)
    y = xf * lax.rsqrt(jnp.mean(xf * xf, axis=-1, keepdims=True) + NORM_EPS)
    return (y * g.astype(F32)).astype(x.dtype)


def modulate(h, shift, scale):
    return h * (1 + scale[:, None, :]) + shift[:, None, :]


def adaln_lora(cond, down, up, bias):
    m = (jax.nn.silu(cond) @ down) @ up + bias
    return jnp.moveaxis(m.reshape(cond.shape[0], N_MOD, D_MODEL), 1, 0)


def swiglu(h, w1, w3, w2):
    return (jax.nn.silu(h @ w1) * (h @ w3)) @ w2


def short_conv3(z, w, b):
    y = lax.conv_general_dilated(z, w[:, None, :], window_strides=(1,), padding=[(1, 1)],
                                 dimension_numbers=('NWC', 'WIO', 'NWC'),
                                 feature_group_count=z.shape[-1])
    return y + b


def bidir_token_shift(z, mu):
    zp = jnp.pad(z, ((0, 0), (1, 1), (0, 0)))
    return z + mu * (0.5 * (zp[:, :-2] + zp[:, 2:]) - z)


def hyena_filter_spectrum(L, w1, b1, w2, b2, freq, w3):
    pos = jnp.arange(L, dtype=F32)[:, None]
    t = pos / max(L - 1, 1)
    bands = jnp.linspace(1e-4, HY_BANDS - 1, HY_BANDS, dtype=F32)[None, :]
    ang = 2 * math.pi * pos / L * bands
    feats = jnp.concatenate([t, jnp.cos(ang), -jnp.sin(ang)], axis=-1)
    h = jnp.sin(freq[0].astype(F32) * (feats @ w1.astype(F32) + b1.astype(F32)))
    h = jnp.sin(freq[1].astype(F32) * (h @ w2.astype(F32) + b2.astype(F32)))
    h = (h @ w3.astype(F32)).reshape(L, 2, HY_ORDER, HY_DIM)
    max_decay = math.log(HY_TARGET) / HY_FAST_PCT
    min_decay = math.log(HY_TARGET) / HY_SLOW_PCT
    deltas = jnp.abs(jnp.linspace(min_decay, max_decay, HY_DIM, dtype=F32))
    h = h * jnp.exp(-t * deltas)[:, None, None, :]
    hf, hb = h[:, 0], h[:, 1]
    taps = jnp.concatenate([hf[:1] + hb[:1], hf[1:], jnp.zeros_like(hf[:1]), hb[:0:-1]], axis=0)
    return jnp.fft.rfft(taps, axis=0)


def long_conv(u, spec):
    L = u.shape[1]
    U = jnp.fft.rfft(u, n=2 * L, axis=1)
    return jnp.fft.irfft(U * spec[None], n=2 * L, axis=1)[:, :L]


def hyena_mixer(z, short_w, short_b, w1, b1, w2, b2, freq, w3, bias):
    L = z.shape[1]
    z = short_conv3(z, short_w, short_b)
    x1, x2, v = jnp.split(z, 3, axis=-1)
    spec = hyena_filter_spectrum(L, w1, b1, w2, b2, freq, w3)
    u = v.astype(F32)
    for o, gate in enumerate((x1, x2)):
        u = gate.astype(F32) * (long_conv(u, spec[:, o]) + bias[o].astype(F32) * u)
    return u.astype(z.dtype)


def _rw_heads(a):
    b, L = a.shape[:2]
    return jnp.moveaxis(a.astype(F32).reshape(b, L, RW_HEADS, RW_HEAD), 1, 0)


def _rw_features(z, mu, w0, w_up, a0, a_up, g_up, k_k, k_a):
    z = bidir_token_shift(z, mu)
    r, k, v, wl, al, gl = jnp.split(z, RW_SPLITS, axis=-1)
    kk = _rw_heads(k * k_k)
    kk = kk * lax.rsqrt(jnp.sum(kk * kk, axis=-1, keepdims=True) + 1e-12)
    g = jax.nn.sigmoid(gl) @ g_up
    dirs = []
    for d in range(2):
        w = jnp.exp(-RW_DECAY_SCALE * jax.nn.sigmoid((w0[d] + jnp.tanh(wl) @ w_up[d]).astype(F32)))
        a = jax.nn.sigmoid((a0[d] + al @ a_up[d]).astype(F32))
        kd = k.astype(F32) * (1 + (a - 1) * k_a.astype(F32))
        dirs.append((_rw_heads(w), _rw_heads(kd), kk * _rw_heads(a)))
    return _rw_heads(r), _rw_heads(v), kk, g, dirs


def rwkv7_scan(s0, r, w, k, v, kk, kka):
    def step(S, inp):
        r_t, w_t, k_t, v_t, kk_t, kka_t = inp
        removed = jnp.einsum('bhvk,bhk->bhv', S, kk_t)
        S = (S * w_t[:, :, None, :] - removed[..., None] * kka_t[:, :, None, :]
             + v_t[..., None] * k_t[:, :, None, :])
        return S, jnp.einsum('bhvk,bhk->bhv', S, r_t)
    return lax.scan(step, s0, (r, w, k, v, kk, kka))


def _rw_out(y, bonus, g, ln_w, ln_b):
    mean = jnp.mean(y, axis=-1, keepdims=True)
    var = jnp.mean(jnp.square(y - mean), axis=-1, keepdims=True)
    y = ((y - mean) * lax.rsqrt(var + RW_GN_EPS) * ln_w.astype(F32).reshape(RW_HEADS, RW_HEAD)
         + ln_b.astype(F32).reshape(RW_HEADS, RW_HEAD))
    y = jnp.moveaxis(y + bonus, 0, 1).reshape(g.shape)
    return (y * g.astype(F32)).astype(g.dtype)


def rwkv7_mixer(zc, zl, mu, w0, w_up, a0, a_up, g_up, k_k, k_a, r_k, ln_w, ln_b, ctx_out):
    feat_args = (mu, w0, w_up, a0, a_up, g_up, k_k, k_a)
    rc, vc, kkc, gc, dirs_c = _rw_features(zc, *feat_args)
    rl, vl, kkl, gl, dirs_l = _rw_features(zl, *feat_args)
    rk = r_k.astype(F32)
    s0 = jnp.zeros((zl.shape[0], RW_HEADS, RW_HEAD, RW_HEAD), F32)
    yc = yl = bc = bl = 0.0
    for d in range(2):
        rev = (lambda a: a[::-1]) if d else (lambda a: a)
        wc, kc, kac = dirs_c[d]
        wl_, kl, kal = dirs_l[d]
        s_ctx, oc = rwkv7_scan(s0, rev(rc), rev(wc), rev(kc), rev(vc), rev(kkc), rev(kac))
        _, ol = rwkv7_scan(s_ctx, rev(rl), rev(wl_), rev(kl), rev(vl), rev(kkl), rev(kal))
        yc, yl = yc + rev(oc), yl + rev(ol)
        bc = bc + jnp.sum(rc * kc * rk, axis=-1, keepdims=True) * vc
        bl = bl + jnp.sum(rl * kl * rk, axis=-1, keepdims=True) * vl
    out_l = _rw_out(yl, bl, gl, ln_w, ln_b)
    out_c = _rw_out(yc, bc, gc, ln_w, ln_b) if ctx_out else None
    return out_c, out_l


def _hg_heads(a):
    b, L = a.shape[:2]
    return a.astype(F32).reshape(b, L, HG_HEADS, HG_HEAD).transpose(0, 2, 1, 3)


def gla_chunk_scan(s0, q, k, v, logf):
    B_, H_, L_, _ = q.shape
    n = L_ // HG_CHUNK

    def chunks(a):
        return jnp.moveaxis(a.reshape(B_, H_, n, HG_CHUNK, a.shape[-1]), 2, 0)

    mask = jnp.tril(jnp.ones((HG_CHUNK, HG_CHUNK), bool))[:, :, None]

    def step(S, inp):
        qc, kc, vc, lc = inp
        b = jnp.cumsum(lc, axis=2)
        rel = jnp.exp(jnp.where(mask, b[:, :, :, None, :] - b[:, :, None, :, :], -jnp.inf))
        att = jnp.einsum('bhtk,bhsk,bhtsk->bhts', qc, kc, rel)
        o = (jnp.einsum('bhts,bhsv->bhtv', att, vc)
             + jnp.einsum('bhtk,bhkv->bhtv', qc * jnp.exp(b), S))
        b_end = b[:, :, -1:, :]
        S = (jnp.exp(b_end[:, :, 0, :, None]) * S
             + jnp.einsum('bhsk,bhsv->bhkv', kc * jnp.exp(b_end - b), vc))
        return S, o

    S, o = lax.scan(step, s0, (chunks(q), chunks(k), chunks(v), chunks(logf)))
    return S, jnp.moveaxis(o, 0, 2).reshape(B_, H_, L_, v.shape[-1])


def axial_rope(rows):
    t = jnp.arange(rows * GRID_W)
    row, col = t // GRID_W, t % GRID_W
    inv = ROPE_THETA ** (-jnp.arange(0, ROPE_AXIS, 2, dtype=F32) / ROPE_AXIS)
    ang = jnp.concatenate([row[:, None] * inv, col[:, None] * inv], axis=-1)
    return jnp.cos(ang)[:, None, :], jnp.sin(ang)[:, None, :]


def apply_rope(x, cos, sin):
    xf = x.astype(F32).reshape(*x.shape[:-1], AT_HEAD // 2, 2)
    x0, x1 = xf[..., 0], xf[..., 1]
    out = jnp.stack([x0 * cos - x1 * sin, x0 * sin + x1 * cos], axis=-1)
    return out.reshape(x.shape).astype(x.dtype)


def gqa_attend(q, k, v):
    B_, Lq = q.shape[:2]
    blk = min(AT_BLOCK, Lq)
    nb = Lq // blk
    qb = q.reshape(B_, nb, blk, AT_KV_HEADS, AT_GROUP, AT_HEAD).transpose(1, 0, 2, 3, 4, 5)
    scale = AT_HEAD ** -0.5

    def one_block(qblk):
        s = jnp.einsum('bqgrd,bkgd->bgrqk', qblk, k).astype(F32) * scale
        p = jax.nn.softmax(s, axis=-1).astype(v.dtype)
        return jnp.einsum('bgrqk,bkgd->bqgrd', p, v)

    o = lax.map(one_block, qb)
    return o.transpose(1, 0, 2, 3, 4, 5).reshape(B_, Lq, AT_Q_HEADS * AT_HEAD)


def even_mixer(hc, hl, w_in, w_out, hy, rw, ctx_out):
    zc, zl = hc @ w_in, hl @ w_in
    n_hy = 3 * HY_DIM
    out_c, out_l = rwkv7_mixer(zc[..., n_hy:], zl[..., n_hy:], *rw, ctx_out=ctx_out)
    yl = jnp.concatenate([hyena_mixer(zl[..., :n_hy], *hy), out_l], axis=-1) @ w_out
    yc = None
    if ctx_out:
        yc = jnp.concatenate([hyena_mixer(zc[..., :n_hy], *hy), out_c], axis=-1) @ w_out
    return yc, yl


def odd_mixer(hc, hl, w_in, w_out, lb, hg_norm_g, q_norm, k_norm, rope, ctx_out):
    pc = jnp.split(hc @ w_in, OD_SPLITS, axis=-1)
    pl = jnp.split(hl @ w_in, OD_SPLITS, axis=-1)

    def hg_prep(p):
        q, ff, fb, i, g = p[:5]
        dirs = []
        for f_pre in (ff, fb):
            f = lb + (1 - lb) * jax.nn.sigmoid(f_pre.astype(F32))
            dirs.append((_hg_heads(1 - f), _hg_heads(jnp.log(f))))
        return _hg_heads(q), _hg_heads(i), g, dirs

    def hg_out(o, g):
        o = rmsnorm(o.transpose(0, 2, 1, 3), hg_norm_g)
        return (o.reshape(g.shape) * jax.nn.silu(g.astype(F32))).astype(g.dtype)

    qc, ic, gc, dc = hg_prep(pc)
    ql, il, gl, dl = hg_prep(pl)
    s0 = jnp.zeros((hl.shape[0], HG_HEADS, HG_HEAD, HG_HEAD), F32)
    oc = ol = 0.0
    for d in range(2):
        rev = (lambda a: jnp.flip(a, 2)) if d else (lambda a: a)
        kc, lfc = dc[d]
        kl, lfl = dl[d]
        s_ctx, o_c = gla_chunk_scan(s0, rev(qc), rev(kc), rev(ic), rev(lfc))
        _, o_l = gla_chunk_scan(s_ctx, rev(ql), rev(kl), rev(il), rev(lfl))
        oc, ol = oc + rev(o_c), ol + rev(o_l)

    def at_prep(p, pos):
        q, k, v = p[5:]
        B_, L_ = q.shape[:2]
        q = rmsnorm(q.reshape(B_, L_, AT_Q_HEADS, AT_HEAD), q_norm)
        k = rmsnorm(k.reshape(B_, L_, AT_KV_HEADS, AT_HEAD), k_norm)
        v = v.reshape(B_, L_, AT_KV_HEADS, AT_HEAD)
        if pos is not None:
            q, k = apply_rope(q, *pos), apply_rope(k, *pos)
        return q, k, v

    aqc, akc, avc = at_prep(pc, None)
    aql, akl, avl = at_prep(pl, rope)
    at_l = gqa_attend(aql, jnp.concatenate([akl, akc], axis=1), jnp.concatenate([avl, avc], axis=1))
    yl = jnp.concatenate([hg_out(ol, gl), at_l], axis=-1) @ w_out
    yc = None
    if ctx_out:
        at_c = gqa_attend(aqc, akc, avc)
        yc = jnp.concatenate([hg_out(oc, gc), at_c], axis=-1) @ w_out
    return yc, yl


def setup_inputs(seed: int = 0) -> dict:
    key = jax.random.key(seed)
    keys = iter(jax.random.split(key, 48))

    def nrm(shape, scale):
        return jax.random.normal(next(keys), shape, F32) * scale

    D = D_MODEL
    inp = {}
    inp['x'] = nrm((BATCH, SEQ, D), 1.0)
    inp['c'] = nrm((BATCH, D), 1.0)
    inp['ctx'] = nrm((BATCH, CTX_LEN, D), 1.0)
    inp['c_ctx'] = nrm((D,), 1.0)
    inp['ada_down'] = nrm((DEPTH, D, ADA_RANK), D ** -0.5)
    inp['ada_up'] = nrm((DEPTH, ADA_RANK, N_MOD * D), 0.3 * ADA_RANK ** -0.5)
    inp['ada_bias'] = nrm((DEPTH, N_MOD * D), 0.02)
    inp['norm_g'] = 1.0 + nrm((DEPTH, 4, D), 0.02)
    inp['ffn_w1'] = nrm((DEPTH, D, FFN_DIM), D ** -0.5)
    inp['ffn_w3'] = nrm((DEPTH, D, FFN_DIM), D ** -0.5)
    inp['ffn_w2'] = nrm((DEPTH, FFN_DIM, D), FFN_DIM ** -0.5)
    inp['ev_w_in'] = nrm((N_EVEN, D, EV_COLS), D ** -0.5)
    inp['ev_w_out'] = nrm((N_EVEN, D, D), D ** -0.5)
    inp['hy_short_w'] = nrm((N_EVEN, HY_SHORT, 3 * HY_DIM), HY_SHORT ** -0.5)
    inp['hy_short_b'] = nrm((N_EVEN, 3 * HY_DIM), 0.02)
    inp['hy_pe_w1'] = nrm((N_EVEN, HY_EMB, HY_FFN), HY_EMB ** -0.5)
    inp['hy_pe_b1'] = nrm((N_EVEN, HY_FFN), 0.1)
    inp['hy_pe_w2'] = nrm((N_EVEN, HY_FFN, HY_FFN), HY_FFN ** -0.5)
    inp['hy_pe_b2'] = nrm((N_EVEN, HY_FFN), 0.1)
    inp['hy_sin_freq'] = 1.0 + nrm((N_EVEN, 2, HY_FFN), 0.1)
    inp['hy_pe_w3'] = nrm((N_EVEN, HY_FFN, 2 * HY_ORDER * HY_DIM), 0.05 * HY_FFN ** -0.5)
    inp['hy_bias'] = nrm((N_EVEN, HY_ORDER, HY_DIM), 0.5)
    inp['rw_mu'] = jax.random.uniform(next(keys), (N_EVEN, RW_COLS), F32)
    inp['rw_w0'] = jnp.linspace(-6.0, -1.0, RW_DIM, dtype=F32) + nrm((N_EVEN, 2, RW_DIM), 0.1)
    inp['rw_w_up'] = nrm((N_EVEN, 2, RW_W_LORA, RW_DIM), 0.5 * RW_W_LORA ** -0.5)
    inp['rw_a0'] = nrm((N_EVEN, 2, RW_DIM), 0.1)
    inp['rw_a_up'] = nrm((N_EVEN, 2, RW_A_LORA, RW_DIM), 0.5 * RW_A_LORA ** -0.5)
    inp['rw_g_up'] = nrm((N_EVEN, RW_G_LORA, RW_DIM), RW_G_LORA ** -0.5)
    inp['rw_k_k'] = 0.85 + nrm((N_EVEN, RW_DIM), 0.02)
    inp['rw_k_a'] = 1.0 + nrm((N_EVEN, RW_DIM), 0.02)
    inp['rw_r_k'] = nrm((N_EVEN, RW_HEADS, RW_HEAD), 0.1)
    inp['rw_ln_w'] = 1.0 + nrm((N_EVEN, RW_DIM), 0.02)
    inp['rw_ln_b'] = nrm((N_EVEN, RW_DIM), 0.02)
    inp['od_w_in'] = nrm((N_ODD, D, OD_COLS), D ** -0.5)
    inp['od_w_out'] = nrm((N_ODD, D, D), D ** -0.5)
    inp['hg_lower_bounds'] = nrm((DEPTH, HG_DIM), 0.1)
    inp['hg_norm_g'] = 1.0 + nrm((N_ODD, HG_HEAD), 0.02)
    inp['at_q_norm'] = 1.0 + nrm((N_ODD, AT_HEAD), 0.02)
    inp['at_k_norm'] = 1.0 + nrm((N_ODD, AT_HEAD), 0.02)
    return inp


def reference(x, c, ctx, c_ctx, ada_down, ada_up, ada_bias, norm_g, ffn_w1, ffn_w3, ffn_w2,
              ev_w_in, ev_w_out, hy_short_w, hy_short_b, hy_pe_w1, hy_pe_b1, hy_pe_w2, hy_pe_b2,
              hy_sin_freq, hy_pe_w3, hy_bias,
              rw_mu, rw_w0, rw_w_up, rw_a0, rw_a_up, rw_g_up, rw_k_k, rw_k_a, rw_r_k, rw_ln_w, rw_ln_b,
              od_w_in, od_w_out, hg_lower_bounds, hg_norm_g, at_q_norm, at_k_norm):
    xl, xc = x, ctx
    rows = x.shape[1] // GRID_W
    rope = axial_rope(rows)
    lb_all = jnp.cumsum(jax.nn.softmax(hg_lower_bounds.astype(F32), axis=0), axis=0)
    lb_all = lb_all - lb_all[0]
    for l in range(DEPTH):
        ctx_out = l < DEPTH - 1
        ml = adaln_lora(c, ada_down[l], ada_up[l], ada_bias[l])
        mc = adaln_lora(c_ctx[None], ada_down[l], ada_up[l], ada_bias[l])
        hl = modulate(rmsnorm(xl, norm_g[l, 0]), ml[0], ml[1])
        hc = modulate(rmsnorm(xc, norm_g[l, 0]), mc[0], mc[1])
        if l % 2 == 0:
            e = l // 2
            hy = (hy_short_w[e], hy_short_b[e], hy_pe_w1[e], hy_pe_b1[e], hy_pe_w2[e], hy_pe_b2[e],
                  hy_sin_freq[e], hy_pe_w3[e], hy_bias[e])
            rw = (rw_mu[e], rw_w0[e], rw_w_up[e], rw_a0[e], rw_a_up[e], rw_g_up[e], rw_k_k[e],
                  rw_k_a[e], rw_r_k[e], rw_ln_w[e], rw_ln_b[e])
            yc, yl = even_mixer(hc, hl, ev_w_in[e], ev_w_out[e], hy, rw, ctx_out)
        else:
            o = l // 2
            yc, yl = odd_mixer(hc, hl, od_w_in[o], od_w_out[o], lb_all[l], hg_norm_g[o],
                               at_q_norm[o], at_k_norm[o], rope, ctx_out)
        xl = xl + ml[2][:, None, :] * rmsnorm(yl, norm_g[l, 1])
        hl = modulate(rmsnorm(xl, norm_g[l, 2]), ml[3], ml[4])
        xl = xl + ml[5][:, None, :] * rmsnorm(swiglu(hl, ffn_w1[l], ffn_w3[l], ffn_w2[l]), norm_g[l, 3])
        if ctx_out:
            xc = xc + mc[2][:, None, :] * rmsnorm(yc, norm_g[l, 1])
            hc = modulate(rmsnorm(xc, norm_g[l, 2]), mc[3], mc[4])
            xc = xc + mc[5][:, None, :] * rmsnorm(swiglu(hc, ffn_w1[l], ffn_w3[l], ffn_w2[l]), norm_g[l, 3])
    return xl
```

```python
import functools
import math

import numpy as np
import jax
import jax.numpy as jnp
from jax import lax
from jax.experimental import pallas as pl
from jax.experimental.pallas import tpu as pltpu

F32 = jnp.float32
BF16 = jnp.bfloat16
HIGHEST = lax.Precision.HIGHEST

NORM_EPS = 1e-6
N_MOD = 6
GRID_W = 64
HY_BANDS = 16
HY_TARGET = 1e-2
HY_FAST_PCT = 0.3
HY_SLOW_PCT = 1.5
RW_HEAD = 64
RW_DECAY_SCALE = math.exp(-0.5)
RW_GN_EPS = 64e-5
HG_HEAD = 128
HG_CHUNK = 64
AT_HEAD = 128
AT_KV_HEADS = 4
ROPE_THETA = 10000.0

LANES = 128
SUBLANES = 8
VMEM_LIMIT = 56 << 20


def _params(*sem):
    return pltpu.CompilerParams(dimension_semantics=sem, vmem_limit_bytes=VMEM_LIMIT)


def _dot(a, b):
    return jnp.dot(a.astype(BF16), b.astype(BF16), preferred_element_type=F32)


def _dot_f32(a, b):
    return jnp.dot(a, b, precision=HIGHEST, preferred_element_type=F32)


def _dot_nt(a, b, exact=False):
    dn = (((1,), (1,)), ((), ()))
    if exact:
        return lax.dot_general(a, b, dn, precision=HIGHEST, preferred_element_type=F32)
    return lax.dot_general(a.astype(BF16), b.astype(BF16), dn, preferred_element_type=F32)


def _dot_tn(a, b, exact=False):
    dn = (((0,), (0,)), ((), ()))
    if exact:
        return lax.dot_general(a, b, dn, precision=HIGHEST, preferred_element_type=F32)
    return lax.dot_general(a.astype(BF16), b.astype(BF16), dn, preferred_element_type=F32)


def _sigmoid(x):
    return jax.nn.sigmoid(x)


def _pick(n, prefs):
    for p in prefs:
        if n % p == 0:
            return p
    return n


def _adaln_kernel(cond_ref, down_ref, up_ref, bias_ref, o_ref):
    cnd = cond_ref[...]
    t = _dot_f32(cnd * _sigmoid(cnd), down_ref[...])
    o_ref[...] = _dot_f32(t, up_ref[...]) + bias_ref[...]


def _adaln(cond8, down, up, bias):
    depth, d, r = down.shape
    n = up.shape[2]
    tn = _pick(n, (2048, 1024, 512, 256, 128))
    return pl.pallas_call(
        _adaln_kernel,
        grid=(depth, n // tn),
        in_specs=[
            pl.BlockSpec((SUBLANES, d), lambda l, j: (0, 0)),
            pl.BlockSpec((None, d, r), lambda l, j: (l, 0, 0)),
            pl.BlockSpec((None, r, tn), lambda l, j: (l, 0, j)),
            pl.BlockSpec((None, 1, tn), lambda l, j: (l, 0, j)),
        ],
        out_specs=pl.BlockSpec((None, SUBLANES, tn), lambda l, j: (l, 0, j)),
        out_shape=jax.ShapeDtypeStruct((depth, SUBLANES, n), F32),
        compiler_params=_params("arbitrary", "arbitrary"),
        name="adaln",
    )(cond8, down, up, bias.reshape(depth, 1, n))


def _rms(x):
    return x * lax.rsqrt(jnp.mean(x * x, axis=-1, keepdims=True) + NORM_EPS)


def _norm_mod_kernel(x_ref, g_ref, mod_ref, o_ref, *, g_row, shift_row, scale_row):
    y = _rms(x_ref[...]) * g_ref[g_row:g_row + 1, :]
    y = y * (1.0 + mod_ref[scale_row:scale_row + 1, :]) + mod_ref[shift_row:shift_row + 1, :]
    o_ref[...] = y.astype(o_ref.dtype)


def _resid_kernel(x_ref, y_ref, g_ref, mod_ref, o_ref, *, g_row, gate_row):
    yn = _rms(y_ref[...]) * g_ref[g_row:g_row + 1, :]
    o_ref[...] = x_ref[...] + mod_ref[gate_row:gate_row + 1, :] * yn


def _row_specs(seq, tr, d, layer):
    n_lat = seq // tr
    tok = pl.BlockSpec((None, tr, d), lambda b, i: (b, i, 0))
    g = pl.BlockSpec((None, 4, d), lambda b, i: (layer, 0, 0))
    mod = pl.BlockSpec((None, None, None, N_MOD, d),
                       lambda b, i: (layer, b, jnp.where(i < n_lat, 0, 1), 0, 0))
    return tok, g, mod


def _norm_mod(xs, norm_g, mods, layer, seq, tr, g_row, shift_row, scale_row):
    b, t, d = xs.shape
    tok, g, mod = _row_specs(seq, tr, d, layer)
    return pl.pallas_call(
        functools.partial(_norm_mod_kernel, g_row=g_row, shift_row=shift_row, scale_row=scale_row),
        grid=(b, t // tr),
        in_specs=[tok, g, mod],
        out_specs=tok,
        out_shape=jax.ShapeDtypeStruct((b, t, d), BF16),
        compiler_params=_params("parallel", "parallel"),
        name="norm_mod",
    )(xs, norm_g, mods)


def _resid(xs, y, norm_g, mods, layer, seq, tr, g_row, gate_row):
    b, t, d = xs.shape
    tok, g, mod = _row_specs(seq, tr, d, layer)
    return pl.pallas_call(
        functools.partial(_resid_kernel, g_row=g_row, gate_row=gate_row),
        grid=(b, t // tr),
        in_specs=[tok, tok, g, mod],
        out_specs=tok,
        out_shape=jax.ShapeDtypeStruct((b, t, d), F32),
        compiler_params=_params("parallel", "parallel"),
        name="resid",
    )(xs, y.reshape(b, t, d), norm_g, mods)


def _mm_kernel(*refs, n_in):
    a_refs, w_refs, o_ref = refs[:n_in], refs[n_in:2 * n_in], refs[2 * n_in]
    acc = _dot(a_refs[0][...], w_refs[0][...])
    for a_ref, w_ref in zip(a_refs[1:], w_refs[1:]):
        acc = acc + _dot(a_ref[...], w_ref[...])
    o_ref[...] = acc.astype(o_ref.dtype)


def _mm(a_list, w, col0, ncols, out_dtype, tm=512, tn=512):
    m = a_list[0].shape[0]
    tm = _pick(m, (tm, 256, 128))
    tn = next(t for t in (tn, 256, 128) if ncols % t == 0 and col0 % t == 0)
    n_in = len(a_list)
    kw = a_list[0].shape[1]
    a_specs = [pl.BlockSpec((tm, kw), lambda i, j: (i, 0)) for _ in a_list]
    w_specs = [pl.BlockSpec((kw, tn), functools.partial(lambda i, j, r: (r, col0 // tn + j), r=r))
               for r in range(n_in)]
    return pl.pallas_call(
        functools.partial(_mm_kernel, n_in=n_in),
        grid=(m // tm, ncols // tn),
        in_specs=a_specs + w_specs,
        out_specs=pl.BlockSpec((tm, tn), lambda i, j: (i, j)),
        out_shape=jax.ShapeDtypeStruct((m, ncols), out_dtype),
        compiler_params=_params("parallel", "parallel"),
        name="proj",
    )(*a_list, *([w] * n_in))


def _swiglu_kernel(a_ref, w1_ref, w3_ref, o_ref):
    a = a_ref[...]
    g = _dot(a, w1_ref[...])
    u = _dot(a, w3_ref[...])
    o_ref[...] = (g * _sigmoid(g) * u).astype(o_ref.dtype)


def _swiglu_up(a, w1, w3, tm=512, tn=256):
    m, k = a.shape
    n = w1.shape[1]
    tm = _pick(m, (tm, 256, 128))
    tn = _pick(n, (tn, 128))
    return pl.pallas_call(
        _swiglu_kernel,
        grid=(m // tm, n // tn),
        in_specs=[pl.BlockSpec((tm, k), lambda i, j: (i, 0)),
                  pl.BlockSpec((k, tn), lambda i, j: (0, j)),
                  pl.BlockSpec((k, tn), lambda i, j: (0, j))],
        out_specs=pl.BlockSpec((tm, tn), lambda i, j: (i, j)),
        out_shape=jax.ShapeDtypeStruct((m, n), BF16),
        compiler_params=_params("parallel", "parallel"),
        name="swiglu_up",
    )(a, w1, w3)


def _mm_acc_kernel(a_ref, w_ref, o_ref, acc_ref):
    k = pl.program_id(2)

    @pl.when(k == 0)
    def _():
        acc_ref[...] = jnp.zeros_like(acc_ref)

    acc_ref[...] += _dot(a_ref[...], w_ref[...])

    @pl.when(k == pl.num_programs(2) - 1)
    def _():
        o_ref[...] = acc_ref[...]


def _mm_ksplit(a, w, tm=512, tn=512):
    m, kdim = a.shape
    n = w.shape[1]
    tm = _pick(m, (tm, 256, 128))
    tn = _pick(n, (tn, 256, 128))
    tk = kdim // 2 if (kdim // 2) % LANES == 0 else kdim
    return pl.pallas_call(
        _mm_acc_kernel,
        grid=(m // tm, n // tn, kdim // tk),
        in_specs=[pl.BlockSpec((tm, tk), lambda i, j, k: (i, k)),
                  pl.BlockSpec((tk, tn), lambda i, j, k: (k, j))],
        out_specs=pl.BlockSpec((tm, tn), lambda i, j, k: (i, j)),
        out_shape=jax.ShapeDtypeStruct((m, n), F32),
        scratch_shapes=[pltpu.VMEM((tm, tn), F32)],
        compiler_params=_params("parallel", "parallel", "arbitrary"),
        name="ffn_down",
    )(a, w)


def _neighbours(x, seq):
    t = x.shape[0]
    row = lax.broadcasted_iota(jnp.int32, x.shape, 0)
    prev = jnp.where(row == 0, 0.0, jnp.where(row == seq, 0.0, pltpu.roll(x, 1, 0)))
    nxt = jnp.where(row == seq - 1, 0.0, jnp.where(row == t - 1, 0.0, pltpu.roll(x, t - 1, 0)))
    return prev, nxt


def _short_conv_kernel(z_ref, p_ref, o_ref, *, seq):
    z = z_ref[...]
    prev, nxt = _neighbours(z, seq)
    o_ref[...] = p_ref[0:1, :] * prev + p_ref[1:2, :] * z + p_ref[2:3, :] * nxt + p_ref[3:4, :]


def _token_shift_kernel(z_ref, p_ref, o_ref, *, seq):
    z = z_ref[...]
    prev, nxt = _neighbours(z, seq)
    o_ref[...] = z + p_ref[...] * (0.5 * (prev + nxt) - z)


def _time_mix(body, z, params, seq, name):
    b, t, cw = z.shape
    tc = LANES
    return pl.pallas_call(
        functools.partial(body, seq=seq),
        grid=(b, cw // tc),
        in_specs=[pl.BlockSpec((None, t, tc), lambda i, j: (i, 0, j)),
                  pl.BlockSpec((params.shape[0], tc), lambda i, j: (0, j))],
        out_specs=pl.BlockSpec((None, t, tc), lambda i, j: (i, 0, j)),
        out_shape=jax.ShapeDtypeStruct((b, t, cw), F32),
        compiler_params=_params("parallel", "parallel"),
        name=name,
    )(z, params)


def _hy_hidden_kernel(bands_ref, w1t_ref, w1c_ref, w1s_ref, b1_ref, w2_ref, b2_ref, freq_ref, o_ref, *, length):
    hp = o_ref.shape[1]
    pos_b = lax.broadcasted_iota(jnp.int32, (length, bands_ref.shape[1]), 0).astype(F32)
    ang = (2 * math.pi) * pos_b / length * bands_ref[...]
    t = lax.broadcasted_iota(jnp.int32, (length, hp), 0).astype(F32) / max(length - 1, 1)
    pre = (t * w1t_ref[...] + _dot_f32(jnp.cos(ang), w1c_ref[...]) + _dot_f32(-jnp.sin(ang), w1s_ref[...])
           + b1_ref[...])
    h = jnp.sin(freq_ref[0:1, :] * pre)
    o_ref[...] = jnp.sin(freq_ref[1:2, :] * (_dot_f32(h, w2_ref[...]) + b2_ref[...]))


def _hy_hidden(length, bands, w1t, w1c, w1s, b1, w2, b2, freq):
    hp = w2.shape[0]
    full = lambda a: pl.BlockSpec(a.shape, lambda i: (0,) * a.ndim)
    args = (bands, w1t, w1c, w1s, b1, w2, b2, freq)
    return pl.pallas_call(
        functools.partial(_hy_hidden_kernel, length=length),
        grid=(1,),
        in_specs=[full(a) for a in args],
        out_specs=pl.BlockSpec((length, hp), lambda i: (0, 0)),
        out_shape=jax.ShapeDtypeStruct((length, hp), F32),
        compiler_params=_params("arbitrary"),
        name="hy_hidden",
    )(*args)


def _hy_taps_kernel(h_ref, w3_ref, delta_ref, o_ref, *, length):
    t = lax.broadcasted_iota(jnp.int32, o_ref.shape, 0).astype(F32) / max(length - 1, 1)
    o_ref[...] = _dot_f32(h_ref[...], w3_ref[...]) * jnp.exp(-t * delta_ref[...])


def _hy_taps(hmid, w3p, delta, c):
    length, hp = hmid.shape
    tc = _pick(c, (256, 128))
    nct = c // tc
    return pl.pallas_call(
        functools.partial(_hy_taps_kernel, length=length),
        grid=(4, nct),
        in_specs=[pl.BlockSpec((length, hp), lambda g, j: (0, 0)),
                  pl.BlockSpec((hp, tc), lambda g, j: (0, g * nct + j)),
                  pl.BlockSpec((1, tc), lambda g, j: (0, j))],
        out_specs=pl.BlockSpec((None, length, tc), lambda g, j: (g, 0, j)),
        out_shape=jax.ShapeDtypeStruct((4, length, c), F32),
        compiler_params=_params("parallel", "parallel"),
        name="hy_taps",
    )(hmid, w3p, delta)


def _fft_tables(n1, n2):
    n = n1 * n2
    n1h = n1 // 2

    def cs(phase, period):
        ang = (phase % period).astype(F32) * (2 * math.pi / period)
        return jnp.cos(ang), jnp.sin(ang)

    k1 = jnp.arange(n1, dtype=jnp.int32)
    c, s = cs(k1[:, None] * jnp.arange(n1h, dtype=jnp.int32)[None, :], n1)
    f_a = jnp.concatenate([c, -s], axis=0)
    c, s = cs(jnp.arange(n1h, dtype=jnp.int32)[:, None] * k1[None, :], n1)
    f_c = jnp.concatenate([c, -s], axis=1) / n
    k2 = jnp.arange(n2, dtype=jnp.int32)
    freq = k1[:, None, None] + n1 * k2[None, :, None]
    c, s = cs(freq * k2[None, None, :], n)
    g = jnp.concatenate([jnp.concatenate([c, s], axis=2),
                         jnp.concatenate([-s, c], axis=2)], axis=1)
    ct, st = jnp.swapaxes(c, 1, 2), jnp.swapaxes(s, 1, 2)
    gh = jnp.concatenate([jnp.concatenate([ct, -st], axis=2),
                          jnp.concatenate([st, ct], axis=2)], axis=1)
    return f_a.astype(BF16), f_c.astype(BF16), g.astype(BF16), gh.astype(BF16)


def _fft_a_kernel(u_ref, f_ref, ar_ref, ai_ref):
    n1 = ar_ref.shape[0]
    res = _dot(f_ref[...], u_ref[...])
    ar_ref[...] = res[:n1].astype(BF16)
    ai_ref[...] = res[n1:].astype(BF16)


def _fft_a(src3, groups, g, c, n2, f_a):
    bn = src3.shape[0]
    n1 = f_a.shape[0] // 2
    out = jax.ShapeDtypeStruct((bn, n1, n2 * c), BF16)
    ospec = pl.BlockSpec((None, n1, c), lambda b, j: (b, 0, j))
    ar, ai = pl.pallas_call(
        _fft_a_kernel,
        grid=(bn, n2),
        in_specs=[pl.BlockSpec((None, n1 // 2, c), lambda b, j: (b, 0, j * groups + g)),
                  pl.BlockSpec(f_a.shape, lambda b, j: (0, 0))],
        out_specs=[ospec, ospec],
        out_shape=[out, out],
        compiler_params=_params("parallel", "parallel"),
        name="fft_a",
    )(src3, f_a)
    return ar.reshape(bn, n1, n2, c), ai.reshape(bn, n1, n2, c)


def _stack(r_ref, i_ref):
    return jnp.concatenate([r_ref[...], i_ref[...]], axis=0)


def _fft_spec_kernel(fr_ref, fi_ref, br_ref, bi_ref, g_ref, kr_ref, ki_ref):
    n2 = kr_ref.shape[0]
    xf = jnp.dot(g_ref[...], _stack(fr_ref, fi_ref), preferred_element_type=F32)
    xb = jnp.dot(g_ref[...], _stack(br_ref, bi_ref), preferred_element_type=F32)
    kr_ref[...] = xf[:n2] + xb[:n2]
    ki_ref[...] = xf[n2:] - xb[n2:]


def _fft_spec(tr, ti, g):
    _, n1, n2, c = tr.shape
    fwd = pl.BlockSpec((None, None, n2, c), lambda o, k: (o, k, 0, 0))
    bwd = pl.BlockSpec((None, None, n2, c), lambda o, k: (2 + o, k, 0, 0))
    out = jax.ShapeDtypeStruct((2, n1, n2, c), F32)
    return pl.pallas_call(
        _fft_spec_kernel,
        grid=(2, n1),
        in_specs=[fwd, fwd, bwd, bwd, pl.BlockSpec((None, 2 * n2, 2 * n2), lambda o, k: (k, 0, 0))],
        out_specs=[fwd, fwd],
        out_shape=[out, out],
        compiler_params=_params("parallel", "parallel"),
        name="fft_spec",
    )(tr, ti, tr, ti, g)


def _fft_mid_kernel(ar_ref, ai_ref, g_ref, gh_ref, kr_ref, ki_ref, zr_ref, zi_ref):
    n2 = zr_ref.shape[0]
    x = jnp.dot(g_ref[...], _stack(ar_ref, ai_ref), preferred_element_type=F32)
    xr, xi = x[:n2], x[n2:]
    kr, ki = kr_ref[...], ki_ref[...]
    y = jnp.concatenate([(xr * kr - xi * ki).astype(BF16), (xr * ki + xi * kr).astype(BF16)], axis=0)
    z = jnp.dot(gh_ref[...], y, preferred_element_type=F32)
    zr_ref[...] = z[:n2].astype(BF16)
    zi_ref[...] = z[n2:].astype(BF16)


def _fft_mid(ar, ai, g, gh, kr, ki, order):
    bn, n1, n2, c = ar.shape
    act = pl.BlockSpec((None, None, n2, c), lambda k, b: (b, k, 0, 0))
    mat = pl.BlockSpec((None, 2 * n2, 2 * n2), lambda k, b: (k, 0, 0))
    spec = pl.BlockSpec((None, None, n2, c), lambda k, b: (order, k, 0, 0))
    out = jax.ShapeDtypeStruct((bn, n1, n2, c), BF16)
    return pl.pallas_call(
        _fft_mid_kernel,
        grid=(n1, bn),
        in_specs=[act, act, mat, mat, spec, spec],
        out_specs=[act, act],
        out_shape=[out, out],
        compiler_params=_params("parallel", "parallel"),
        name="fft_mid",
    )(ar, ai, g, gh, kr, ki)


def _fft_c_kernel(zr_ref, zi_ref, f_ref, u_ref, gate_ref, bias_ref, o_ref):
    conv = jnp.dot(f_ref[...], _stack(zr_ref, zi_ref), preferred_element_type=F32)
    u = u_ref[...]
    o_ref[...] = gate_ref[...] * (conv + bias_ref[...] * u)


def _fft_c(zr, zi, f_c, u3, u_groups, u_g, gate3, gate_groups, gate_g, bias, order):
    bn, n1, n2, c = zr.shape
    zspec = pl.BlockSpec((None, n1, c), lambda b, j: (b, 0, j))
    tok = lambda groups, g: pl.BlockSpec((None, n1 // 2, c), lambda b, j: (b, 0, j * groups + g))
    return pl.pallas_call(
        _fft_c_kernel,
        grid=(bn, n2),
        in_specs=[zspec, zspec, pl.BlockSpec(f_c.shape, lambda b, j: (0, 0)), tok(u_groups, u_g),
                  tok(gate_groups, gate_g), pl.BlockSpec((None, 1, c), lambda b, j: (order, 0, 0))],
        out_specs=tok(1, 0),
        out_shape=jax.ShapeDtypeStruct((bn, n1 // 2, n2 * c), F32),
        compiler_params=_params("parallel", "parallel"),
        name="fft_c",
    )(zr.reshape(bn, n1, n2 * c), zi.reshape(bn, n1, n2 * c), f_c, u3, gate3, bias)


def _hy_ctx_kernel(x1_ref, x2_ref, v_ref, taps_ref, bias_ref, fwd_ref, inv_ref, o_ref):
    n = fwd_ref.shape[0] // 2
    fwd, inv = fwd_ref[...], inv_ref[...]

    def spec(a):
        s = _dot_f32(fwd, a)
        return s[:n], s[n:]

    u = v_ref[...]
    for o, gate_ref in ((0, x1_ref), (1, x2_ref)):
        fr, fi = spec(taps_ref[o])
        br, bi = spec(taps_ref[2 + o])
        kr, ki = fr + br, fi - bi
        ur, ui = spec(u)
        y = jnp.concatenate([ur * kr - ui * ki, ur * ki + ui * kr], axis=0)
        u = gate_ref[...] * (_dot_f32(inv, y) + bias_ref[o:o + 1, :] * u)
    o_ref[...] = u


def _hy_ctx(zc, taps, bias, seq, ctx_len, c):
    b = zc.shape[0]
    n = 2 * ctx_len
    tc = _pick(c, (256, 128))
    nct = c // tc
    k = jnp.arange(n, dtype=jnp.int32)[:, None]
    m = jnp.arange(ctx_len, dtype=jnp.int32)[None, :]
    ang = ((k * m) % n).astype(F32) * (2 * math.pi / n)
    fwd = jnp.concatenate([jnp.cos(ang), -jnp.sin(ang)], axis=0)
    inv = jnp.concatenate([jnp.cos(ang).T, -jnp.sin(ang).T], axis=1) / n
    row = seq // ctx_len
    tok = lambda g: pl.BlockSpec((None, ctx_len, tc), lambda j, i: (i, row, g * nct + j))
    return pl.pallas_call(
        _hy_ctx_kernel,
        grid=(nct, b),
        in_specs=[tok(0), tok(1), tok(2),
                  pl.BlockSpec((4, ctx_len, tc), lambda j, i: (0, 0, j)),
                  pl.BlockSpec((2, tc), lambda j, i: (0, j)),
                  pl.BlockSpec(fwd.shape, lambda j, i: (0, 0)),
                  pl.BlockSpec(inv.shape, lambda j, i: (0, 0))],
        out_specs=pl.BlockSpec((None, ctx_len, tc), lambda j, i: (i, 0, j)),
        out_shape=jax.ShapeDtypeStruct((b, ctx_len, c), F32),
        compiler_params=_params("parallel", "parallel"),
        name="hy_ctx",
    )(zc, zc, zc, taps, bias, fwd, inv)


def _hyena(zh, hy, seq, ctx_len, tables):
    short_w, short_b, w1, b1, w2, b2, freq, w3, bias = hy
    b, t, c3 = zh.shape
    c = c3 // 3
    f_a, f_c, g, gh = tables
    n2 = LANES
    zc = _time_mix(_short_conv_kernel, zh, jnp.concatenate([short_w, short_b[None]], axis=0), seq, "short_conv")

    hf = w2.shape[0]
    hp = LANES
    pad_c = lambda a: jnp.pad(a, ((0, 0), (0, hp - hf)))
    bands = jnp.pad(jnp.linspace(1e-4, HY_BANDS - 1, HY_BANDS, dtype=F32)[None, :], ((0, 0), (0, LANES - HY_BANDS)))
    w1c = jnp.pad(w1[1:1 + HY_BANDS], ((0, LANES - HY_BANDS), (0, hp - hf)))
    w1s = jnp.pad(w1[1 + HY_BANDS:], ((0, LANES - HY_BANDS), (0, hp - hf)))
    hid_args = (bands, pad_c(w1[0:1]), w1c, w1s, pad_c(b1[None]), jnp.pad(w2, ((0, hp - hf), (0, hp - hf))),
                pad_c(b2[None]), pad_c(freq))
    w3p = jnp.pad(w3, ((0, hp - hf), (0, 0)))
    max_decay = math.log(HY_TARGET) / HY_FAST_PCT
    min_decay = math.log(HY_TARGET) / HY_SLOW_PCT
    delta = jnp.abs(jnp.linspace(min_decay, max_decay, c, dtype=F32))[None, :]

    taps_l = _hy_taps(_hy_hidden(seq, *hid_args), w3p, delta, c)
    taps_c = _hy_taps(_hy_hidden(ctx_len, *hid_args), w3p, delta, c)

    tr, ti = _fft_a(taps_l.reshape(4, seq // n2, n2 * c), 1, 0, c, n2, f_a)
    kr, ki = _fft_spec(tr, ti, g)

    zc3 = zc.reshape(b, t // n2, n2 * c3)
    bias3 = bias.reshape(2, 1, c)
    ar, ai = _fft_a(zc3, 3, 2, c, n2, f_a)
    zr, zi = _fft_mid(ar, ai, g, gh, kr, ki, 0)
    u1 = _fft_c(zr, zi, f_c, zc3, 3, 2, zc3, 3, 0, bias3, 0)
    ar, ai = _fft_a(u1, 1, 0, c, n2, f_a)
    zr, zi = _fft_mid(ar, ai, g, gh, kr, ki, 1)
    u2 = _fft_c(zr, zi, f_c, u1, 1, 0, zc3, 3, 1, bias3, 1)
    out_c = _hy_ctx(zc, taps_c, bias, seq, ctx_len, c)
    return jnp.concatenate([u2.reshape(b, seq, c), out_c], axis=1)


def _head_sum(x, width):
    r = lax.broadcasted_iota(jnp.int32, (LANES, LANES), 0) // width
    c = lax.broadcasted_iota(jnp.int32, (LANES, LANES), 1) // width
    ones_blk = jnp.where(r == c, 1.0, 0.0).astype(F32)
    parts = [_dot_f32(x[:, s:s + LANES], ones_blk) for s in range(0, x.shape[1], LANES)]
    return parts[0] if len(parts) == 1 else jnp.concatenate(parts, axis=1)


def _rw_feat_kernel(r_ref, k_ref, v_ref, zl_ref, wup_ref, aup_ref, gup_ref, w0_ref, a0_ref, kkp_ref, kap_ref,
                    rk_ref, kk_o, g_o, bonus_o, w_o, kd_o, kka_o):
    r, k, v = r_ref[...], k_ref[...], v_ref[...]
    zl = zl_ref[...]
    wl, al, gl = zl[:, :LANES], zl[:, LANES:2 * LANES], zl[:, 2 * LANES:]
    kq = k * kkp_ref[...]
    kk = kq * lax.rsqrt(_head_sum(kq * kq, RW_HEAD) + 1e-12)
    kk_o[...] = kk
    g_o[...] = _dot(_sigmoid(gl), gup_ref[...])
    twl = jnp.tanh(wl)
    bonus = jnp.zeros_like(v)
    for d in range(2):
        w = jnp.exp(-RW_DECAY_SCALE * _sigmoid(w0_ref[d:d + 1, :] + _dot_f32(twl, wup_ref[d])))
        a = _sigmoid(a0_ref[d:d + 1, :] + _dot_f32(al, aup_ref[d]))
        kd = k * (1.0 + (a - 1.0) * kap_ref[...])
        w_o[d] = w
        kd_o[d] = kd
        kka_o[d] = kk * a
        bonus = bonus + _head_sum(r * kd * rk_ref[...], RW_HEAD) * v
    bonus_o[...] = bonus


def _rw_features(zs, zl, w_up, a_up, g_up, w0, a0, k_k, k_a, r_k, tr):
    b, t, c3 = zs.shape
    c = c3 // 3
    tc = _pick(c, (512, 256, 128))
    nct = c // tc
    tok = lambda g: pl.BlockSpec((None, tr, tc), lambda i, j, n: (i, j, g * nct + n))
    vec = lambda rows: pl.BlockSpec((rows, tc), lambda i, j, n: (0, n))
    lora = lambda a: pl.BlockSpec((2, a.shape[1], tc), lambda i, j, n: (0, 0, n))
    one = pl.BlockSpec((None, tr, tc), lambda i, j, n: (i, j, n))
    two = pl.BlockSpec((2, None, tr, tc), lambda i, j, n: (0, i, j, n))
    s1 = jax.ShapeDtypeStruct((b, t, c), F32)
    s2 = jax.ShapeDtypeStruct((2, b, t, c), F32)
    return pl.pallas_call(
        _rw_feat_kernel,
        grid=(b, t // tr, nct),
        in_specs=[tok(0), tok(1), tok(2), pl.BlockSpec((None, tr, zl.shape[2]), lambda i, j, n: (i, j, 0)),
                  lora(w_up), lora(a_up), pl.BlockSpec((g_up.shape[0], tc), lambda i, j, n: (0, n)),
                  vec(2), vec(2), vec(1), vec(1), vec(1)],
        out_specs=[one, one, one, two, two, two],
        out_shape=[s1, s1, s1, s2, s2, s2],
        compiler_params=_params("parallel", "parallel", "parallel"),
        name="rw_features",
    )(zs, zs, zs, zl, w_up, a_up, g_up, w0, a0, k_k, k_a, r_k)


RW_CHUNK = LANES
RW_PAIRS = 2


def _rw_scan_kernel(r_ref, w_ref, k_ref, v_ref, kk_ref, kka_ref, y_ref, s_ref, vt_ref, yt_ref, *, reverse):
    n_heads = s_ref.shape[0]

    @pl.when(pl.program_id(2) == 0)
    def _():
        s_ref[...] = jnp.zeros_like(s_ref)

    for p in range(n_heads // 2):
        vt_ref[p] = v_ref[:, p * LANES:(p + 1) * LANES].T
    lane = lax.broadcasted_iota(jnp.int32, (RW_HEAD, LANES), 1)
    lane_row = lax.broadcasted_iota(jnp.int32, (1, LANES), 1)

    def group(i, carry):
        g = (RW_CHUNK // SUBLANES - 1 - i) if reverse else i
        base = pl.multiple_of(g * SUBLANES, SUBLANES)
        tiles = [ref[pl.ds(base, SUBLANES), :] for ref in (kk_ref, kka_ref, k_ref, w_ref, r_ref)]
        state = list(carry)
        for j in range(SUBLANES):
            jj = SUBLANES - 1 - j if reverse else j
            at_t = lane == base + jj
            for h in range(n_heads):
                p, hb = divmod(h, 2)
                sl = slice(p * LANES, (p + 1) * LANES)
                rows = slice(hb * RW_HEAD, (hb + 1) * RW_HEAD)
                own = (lane_row < RW_HEAD) if hb == 0 else (lane_row >= RW_HEAD)
                kk, kka, kd, w, r = (tile[jj:jj + 1, sl] for tile in tiles)
                s = state[h]
                removed = jnp.sum(s * kk, axis=-1, keepdims=True)
                vcol = jnp.sum(jnp.where(at_t, vt_ref[p, rows, :], 0.0), axis=-1, keepdims=True)
                s = s * w - removed * jnp.where(own, kka, 0.0) + vcol * jnp.where(own, kd, 0.0)
                y = jnp.sum(s * r, axis=-1, keepdims=True)
                yt_ref[p, rows, :] = jnp.where(at_t, y, yt_ref[p, rows, :])
                state[h] = s
        return tuple(state)

    fin = lax.fori_loop(0, RW_CHUNK // SUBLANES, group, tuple(s_ref[h] for h in range(n_heads)))
    for h in range(n_heads):
        s_ref[h] = fin[h]
    for p in range(n_heads // 2):
        y_ref[:, p * LANES:(p + 1) * LANES] = yt_ref[p].T


def _segment_order(n_lat, n_ctx, reverse):
    if reverse:
        return lambda j: n_lat + n_ctx - 1 - j
    return lambda j: jnp.where(j < n_ctx, n_lat + j, j - n_ctx)


def _rw_scan(zs, w, kd, kk, kka, seq, direction):
    b, t, c = kk.shape
    wl = RW_PAIRS * LANES
    ng = c // wl
    reverse = direction == 1
    order = _segment_order(seq // RW_CHUNK, (t - seq) // RW_CHUNK, reverse)
    blk = lambda g: pl.BlockSpec((None, RW_CHUNK, wl), lambda i, n, j: (i, order(j), g * ng + n))
    dblk = pl.BlockSpec((None, None, RW_CHUNK, wl), lambda i, n, j: (direction, i, order(j), n))
    return pl.pallas_call(
        functools.partial(_rw_scan_kernel, reverse=reverse),
        grid=(b, ng, t // RW_CHUNK),
        in_specs=[blk(0), dblk, dblk, blk(2), blk(0), dblk],
        out_specs=blk(0),
        out_shape=jax.ShapeDtypeStruct((b, t, c), F32),
        scratch_shapes=[pltpu.VMEM((2 * RW_PAIRS, RW_HEAD, LANES), F32),
                        pltpu.VMEM((RW_PAIRS, LANES, RW_CHUNK), F32),
                        pltpu.VMEM((RW_PAIRS, LANES, RW_CHUNK), F32)],
        compiler_params=_params("parallel", "parallel", "arbitrary"),
        name="rw_scan",
    )(zs, w, kd, zs, kk, kka)


def _rw_out_kernel(y0_ref, y1_ref, bonus_ref, g_ref, lnw_ref, lnb_ref, o_ref):
    y = y0_ref[...] + y1_ref[...]
    mean = _head_sum(y, RW_HEAD) * (1.0 / RW_HEAD)
    yc = y - mean
    var = _head_sum(yc * yc, RW_HEAD) * (1.0 / RW_HEAD)
    yn = yc * lax.rsqrt(var + RW_GN_EPS) * lnw_ref[...] + lnb_ref[...]
    o_ref[...] = ((yn + bonus_ref[...]) * g_ref[...]).astype(o_ref.dtype)


def _rw_out(y0, y1, bonus, g, ln_w, ln_b, tr):
    b, t, c = y0.shape
    tc = _pick(c, (512, 256, 128))
    tok = pl.BlockSpec((None, tr, tc), lambda i, j, n: (i, j, n))
    vec = pl.BlockSpec((1, tc), lambda i, j, n: (0, n))
    return pl.pallas_call(
        _rw_out_kernel,
        grid=(b, t // tr, c // tc),
        in_specs=[tok, tok, tok, tok, vec, vec],
        out_specs=tok,
        out_shape=jax.ShapeDtypeStruct((b, t, c), BF16),
        compiler_params=_params("parallel", "parallel", "parallel"),
        name="rw_out",
    )(y0, y1, bonus, g, ln_w, ln_b)


def _pad_rows(a, rows):
    return jnp.pad(a, [(0, 0)] * (a.ndim - 2) + [(0, rows - a.shape[-2]), (0, 0)])


def _rwkv(zs_raw, zl_raw, rw, seq, tr):
    mu, w0, w_up, a0, a_up, g_up, k_k, k_a, r_k, ln_w, ln_b = rw
    c = k_k.shape[0]
    lw, la = w_up.shape[1], a_up.shape[1]
    mu_l = mu[3 * c:]
    pad_to = lambda a: jnp.pad(a, (0, LANES - a.shape[0]))
    mu_lora = jnp.concatenate([pad_to(mu_l[:lw]), pad_to(mu_l[lw:lw + la]), mu_l[lw + la:]])
    zs = _time_mix(_token_shift_kernel, zs_raw, mu[None, :3 * c], seq, "token_shift")
    zl = _time_mix(_token_shift_kernel, zl_raw, mu_lora[None, :], seq, "token_shift")
    kk, g, bonus, w, kd, kka = _rw_features(
        zs, zl, _pad_rows(w_up, LANES), _pad_rows(a_up, LANES), g_up.astype(BF16), w0, a0,
        k_k[None], k_a[None], r_k.reshape(1, c), tr)
    y0 = _rw_scan(zs, w, kd, kk, kka, seq, 0)
    y1 = _rw_scan(zs, w, kd, kk, kka, seq, 1)
    return _rw_out(y0, y1, bonus, g, ln_w[None], ln_b[None], tr)


def _hg_bounds_kernel(x_ref, o_ref):
    x = x_ref[...]
    e = jnp.exp(x - jnp.max(x, axis=0, keepdims=True))
    p = e / jnp.sum(e, axis=0, keepdims=True)
    run = jnp.zeros_like(p[0:1])
    o_ref[0:1, :] = run
    for i in range(1, x.shape[0]):
        run = run + p[i:i + 1]
        o_ref[i:i + 1, :] = run


def _hg_bounds(lower):
    return pl.pallas_call(
        _hg_bounds_kernel,
        out_shape=jax.ShapeDtypeStruct(lower.shape, F32),
        name="hg_bounds",
    )(lower)


HG_HEADS_PER_STEP = 4


def _hg_scan_kernel(q_ref, f_ref, i_ref, lb_ref, o_ref, s_ref, *, reverse):
    cs = q_ref.shape[0]
    n_heads = s_ref.shape[0]

    @pl.when(pl.program_id(2) == 0)
    def _():
        s_ref[...] = jnp.zeros_like(s_ref)

    row = lax.broadcasted_iota(jnp.int32, (cs, cs), 0)
    col = lax.broadcasted_iota(jnp.int32, (cs, cs), 1)
    row_l = lax.broadcasted_iota(jnp.int32, (cs, HG_HEAD), 0)
    cum = jnp.where((col >= row) if reverse else (col <= row), 1.0, 0.0)
    levels = []
    hs = cs // 2
    while hs >= 1:
        blk = 2 * hs
        boundary = (row & ~(blk - 1)) + hs
        sel = jnp.where(col == (boundary if reverse else boundary - 1), 1.0, 0.0)
        same = (row & ~(blk - 1)) == (col & ~(blk - 1))
        late = (row_l & (blk - 1)) >= hs
        levels.append((sel, same, late))
        hs //= 2

    for h in range(n_heads):
        sl = slice(h * HG_HEAD, (h + 1) * HG_HEAD)
        q, v, lb = q_ref[:, sl], i_ref[:, sl], lb_ref[:, sl]
        f = lb + (1.0 - lb) * _sigmoid(f_ref[:, sl])
        k = 1.0 - f
        b = _dot_f32(cum, jnp.log(f))
        att = jnp.where(row == col, jnp.sum(q * k, axis=-1, keepdims=True), 0.0)
        for sel, same, late in levels:
            ref_b = _dot_f32(sel, b)
            q_side, k_side = (jnp.logical_not(late), late) if reverse else (late, jnp.logical_not(late))
            qt = q * jnp.exp(jnp.where(q_side, b - ref_b, -jnp.inf))
            kt = k * jnp.exp(jnp.where(k_side, ref_b - b, -jnp.inf))
            att = att + jnp.where(same, _dot_nt(qt, kt), 0.0)
        st = s_ref[h]
        b_end = b[0:1, :] if reverse else b[cs - 1:cs, :]
        o_ref[:, sl] = _dot(att, v) + _dot_nt(q * jnp.exp(b), st)
        s_ref[h] = jnp.exp(b_end) * st + _dot_tn(v, k * jnp.exp(b_end - b))


def _hg_scan(p, lb, c, seq, direction):
    b, t, _ = p.shape
    wl = HG_HEADS_PER_STEP * HG_HEAD
    ng = c // wl
    reverse = direction == 1
    order = _segment_order(seq // HG_CHUNK, (t - seq) // HG_CHUNK, reverse)
    blk = lambda g: pl.BlockSpec((None, HG_CHUNK, wl), lambda i, n, j: (i, order(j), g * ng + n))
    return pl.pallas_call(
        functools.partial(_hg_scan_kernel, reverse=reverse),
        grid=(b, ng, t // HG_CHUNK),
        in_specs=[blk(0), blk(1 + direction), blk(3), pl.BlockSpec((1, wl), lambda i, n, j: (0, n))],
        out_specs=blk(0),
        out_shape=jax.ShapeDtypeStruct((b, t, c), F32),
        scratch_shapes=[pltpu.VMEM((HG_HEADS_PER_STEP, HG_HEAD, HG_HEAD), F32)],
        compiler_params=_params("parallel", "parallel", "arbitrary"),
        name="hg_scan",
    )(p, p, p, lb)


def _hg_out_kernel(o0_ref, o1_ref, g_ref, w_ref, out_ref):
    o = o0_ref[...] + o1_ref[...]
    g = g_ref[...]
    for s in range(0, o.shape[1], HG_HEAD):
        oh = o[:, s:s + HG_HEAD]
        gh = g[:, s:s + HG_HEAD]
        out_ref[:, s:s + HG_HEAD] = (_rms(oh) * w_ref[...] * (gh * _sigmoid(gh))).astype(out_ref.dtype)


def _hg_out(o0, o1, p, norm_w, tr):
    b, t, c = o0.shape
    tc = _pick(c, (512, 256, 128))
    nct = c // tc
    tok = pl.BlockSpec((None, tr, tc), lambda i, j, n: (i, j, n))
    return pl.pallas_call(
        _hg_out_kernel,
        grid=(b, t // tr, nct),
        in_specs=[tok, tok, pl.BlockSpec((None, tr, tc), lambda i, j, n: (i, j, 4 * nct + n)),
                  pl.BlockSpec((1, HG_HEAD), lambda i, j, n: (0, 0))],
        out_specs=tok,
        out_shape=jax.ShapeDtypeStruct((b, t, c), BF16),
        compiler_params=_params("parallel", "parallel", "parallel"),
        name="hg_out",
    )(o0, o1, p, norm_w)


def _rope_tables(seq):
    t = jnp.arange(seq)
    axis = AT_HEAD // 2
    inv = ROPE_THETA ** (-jnp.arange(0, axis, 2, dtype=F32) / axis)
    ang = jnp.concatenate([(t // GRID_W)[:, None] * inv, (t % GRID_W)[:, None] * inv], axis=-1)
    sign = jnp.tile(jnp.array([-1.0, 1.0], F32), AT_HEAD // 2)
    return jnp.repeat(jnp.cos(ang), 2, axis=-1), jnp.repeat(jnp.sin(ang), 2, axis=-1) * sign


def _at_prep_kernel(q_ref, k_ref, qw_ref, kw_ref, cos_ref, sin_ref, qo_ref, ko_ref, *, n_lat):
    is_lat = pl.program_id(1) < n_lat
    cos, sin = cos_ref[...], sin_ref[...]
    even = (lax.broadcasted_iota(jnp.int32, cos.shape, 1) & 1) == 0

    def head(x, w):
        y = _rms(x) * w
        partner = jnp.where(even, pltpu.roll(y, AT_HEAD - 1, 1), pltpu.roll(y, 1, 1))
        return jnp.where(is_lat, y * cos + partner * sin, y)

    scale = AT_HEAD ** -0.5
    for s in range(0, qo_ref.shape[1], AT_HEAD):
        qo_ref[:, s:s + AT_HEAD] = (head(q_ref[:, s:s + AT_HEAD], qw_ref[...]) * scale).astype(qo_ref.dtype)
    for s in range(0, ko_ref.shape[1], AT_HEAD):
        ko_ref[:, s:s + AT_HEAD] = head(k_ref[:, s:s + AT_HEAD], kw_ref[...]).astype(ko_ref.dtype)


def _at_prep(p, c, q_norm, k_norm, cos, sin, seq, tr):
    b, t, _ = p.shape
    kvw = AT_KV_HEADS * AT_HEAD
    n_lat = seq // tr
    tab = pl.BlockSpec((tr, AT_HEAD), lambda i, j: (jnp.minimum(j, n_lat - 1), 0))
    vec = pl.BlockSpec((1, AT_HEAD), lambda i, j: (0, 0))
    return pl.pallas_call(
        functools.partial(_at_prep_kernel, n_lat=n_lat),
        grid=(b, t // tr),
        in_specs=[pl.BlockSpec((None, tr, c), lambda i, j: (i, j, 5)),
                  pl.BlockSpec((None, tr, kvw), lambda i, j: (i, j, 6 * c // kvw)),
                  vec, vec, tab, tab],
        out_specs=[pl.BlockSpec((None, tr, c), lambda i, j: (i, j, 0)),
                   pl.BlockSpec((None, tr, kvw), lambda i, j: (i, j, 0))],
        out_shape=[jax.ShapeDtypeStruct((b, t, c), BF16), jax.ShapeDtypeStruct((b, t, kvw), BF16)],
        compiler_params=_params("parallel", "parallel"),
        name="at_prep",
    )(p, p, q_norm, k_norm, cos, sin)


def _flash_kernel(q_ref, k_ref, v_ref, o_ref, m_ref, l_ref, acc_ref):
    kv = pl.program_id(3)

    @pl.when(kv == 0)
    def _():
        m_ref[...] = jnp.full_like(m_ref, -jnp.inf)
        l_ref[...] = jnp.zeros_like(l_ref)
        acc_ref[...] = jnp.zeros_like(acc_ref)

    k = k_ref[...]
    v = v_ref[...].astype(BF16)
    group = m_ref.shape[0]
    for r in range(group):
        s = _dot_nt(q_ref[:, r * AT_HEAD:(r + 1) * AT_HEAD], k)
        m_prev = m_ref[r]
        m_cur = jnp.maximum(m_prev, jnp.max(s, axis=-1, keepdims=True))
        alpha = jnp.exp(m_prev - m_cur)
        p = jnp.exp(s - m_cur[:, 0:1])
        l_ref[r] = alpha * l_ref[r] + jnp.sum(p, axis=-1, keepdims=True)
        acc_ref[r] = alpha * acc_ref[r] + _dot(p, v)
        m_ref[r] = m_cur

    @pl.when(kv == pl.num_programs(3) - 1)
    def _():
        for r in range(group):
            o_ref[:, r * AT_HEAD:(r + 1) * AT_HEAD] = (acc_ref[r] / l_ref[r]).astype(o_ref.dtype)


def _flash(qn, kn, p, c, q_blk0, tq, nq, kv_blk0, tk, nk):
    b = qn.shape[0]
    group = c // AT_HEAD // AT_KV_HEADS
    gw = group * AT_HEAD
    v_col0 = (6 * c + AT_KV_HEADS * AT_HEAD) // AT_HEAD
    return pl.pallas_call(
        _flash_kernel,
        grid=(b, AT_KV_HEADS, nq, nk),
        in_specs=[pl.BlockSpec((None, tq, gw), lambda i, g, a, n: (i, q_blk0 + a, g)),
                  pl.BlockSpec((None, tk, AT_HEAD), lambda i, g, a, n: (i, kv_blk0 + n, g)),
                  pl.BlockSpec((None, tk, AT_HEAD), lambda i, g, a, n: (i, kv_blk0 + n, v_col0 + g))],
        out_specs=pl.BlockSpec((None, tq, gw), lambda i, g, a, n: (i, a, g)),
        out_shape=jax.ShapeDtypeStruct((b, nq * tq, c), BF16),
        scratch_shapes=[pltpu.VMEM((group, tq, AT_HEAD), F32)] * 3,
        compiler_params=_params("parallel", "parallel", "parallel", "arbitrary"),
        name="flash",
    )(qn, kn, p)


def kernel(x, c, ctx, c_ctx, ada_down, ada_up, ada_bias, norm_g, ffn_w1, ffn_w3, ffn_w2, ev_w_in, ev_w_out, hy_short_w, hy_short_b, hy_pe_w1, hy_pe_b1, hy_pe_w2, hy_pe_b2, hy_sin_freq, hy_pe_w3, hy_bias, rw_mu, rw_w0, rw_w_up, rw_a0, rw_a_up, rw_g_up, rw_k_k, rw_k_a, rw_r_k, rw_ln_w, rw_ln_b, od_w_in, od_w_out, hg_lower_bounds, hg_norm_g, at_q_norm, at_k_norm):
    bsz, seq, d = x.shape
    ctx_len = ctx.shape[1]
    t = seq + ctx_len
    depth = ada_down.shape[0]
    half = d // 2
    tr = min(256, ctx_len)
    assert seq % ctx_len == 0 and ctx_len % LANES == 0 and bsz < SUBLANES

    xs = jnp.concatenate([x, ctx], axis=1)
    cond = jnp.concatenate([c, c_ctx[None], jnp.zeros((SUBLANES - bsz - 1, d), F32)], axis=0)
    m = _adaln(cond, ada_down, ada_up, ada_bias).reshape(depth, SUBLANES, N_MOD, d)
    mods = jnp.stack([m[:, :bsz], jnp.broadcast_to(m[:, bsz:bsz + 1], (depth, bsz, N_MOD, d))], axis=2)

    n1 = 2 * seq // LANES
    tables = _fft_tables(n1, LANES)
    cos, sin = _rope_tables(seq)
    lb_all = _hg_bounds(hg_lower_bounds)

    for l in range(depth):
        h = _norm_mod(xs, norm_g, mods, l, seq, tr, 0, 0, 1).reshape(bsz * t, d)
        if l % 2 == 0:
            e = l // 2
            w_in = ev_w_in[e]
            n_hy = 3 * half
            lw, la = rw_w_up.shape[2], rw_a_up.shape[2]
            w_lo = w_in[:, 2 * n_hy:]
            pad_c = lambda a: jnp.pad(a, ((0, 0), (0, LANES - a.shape[1])))
            w_lora = jnp.concatenate([pad_c(w_lo[:, :lw]), pad_c(w_lo[:, lw:lw + la]), w_lo[:, lw + la:]], axis=1)
            w_main = w_in[:, :2 * n_hy].astype(BF16)
            zh = _mm([h], w_main, 0, n_hy, F32).reshape(bsz, t, n_hy)
            zs = _mm([h], w_main, n_hy, n_hy, F32).reshape(bsz, t, n_hy)
            zl = _mm([h], w_lora.astype(BF16), 0, w_lora.shape[1], F32).reshape(bsz, t, -1)
            hy = (hy_short_w[e], hy_short_b[e], hy_pe_w1[e], hy_pe_b1[e], hy_pe_w2[e], hy_pe_b2[e],
                  hy_sin_freq[e], hy_pe_w3[e], hy_bias[e])
            rw = (rw_mu[e], rw_w0[e], rw_w_up[e], rw_a0[e], rw_a_up[e], rw_g_up[e], rw_k_k[e],
                  rw_k_a[e], rw_r_k[e], rw_ln_w[e], rw_ln_b[e])
            mix_a = _hyena(zh, hy, seq, ctx_len, tables).astype(BF16)
            mix_b = _rwkv(zs, zl, rw, seq, tr)
            w_out = ev_w_out[e]
        else:
            o = l // 2
            p = _mm([h], od_w_in[o].astype(BF16), 0, od_w_in.shape[2], F32).reshape(bsz, t, -1)
            lb = lb_all[l][None, :]
            o0 = _hg_scan(p, lb, half, seq, 0)
            o1 = _hg_scan(p, lb, half, seq, 1)
            mix_a = _hg_out(o0, o1, p, hg_norm_g[o][None], tr)
            qn, kn = _at_prep(p, half, at_q_norm[o][None], at_k_norm[o][None], cos, sin, seq, tr)
            tq = _pick(seq, (512, 256, 128))
            tk = _pick(t, (768, 512, 256, 128))
            at_l = _flash(qn, kn, p, half, 0, tq, seq // tq, 0, tk, t // tk)
            at_c = _flash(qn, kn, p, half, seq // ctx_len, ctx_len, 1, seq // ctx_len, ctx_len, 1)
            mix_b = jnp.concatenate([at_l, at_c], axis=1)
            w_out = od_w_out[o]
        y = _mm([mix_a.reshape(bsz * t, half), mix_b.reshape(bsz * t, half)], w_out.astype(BF16), 0, d, F32)
        xs = _resid(xs, y, norm_g, mods, l, seq, tr, 1, 2)
        h = _norm_mod(xs, norm_g, mods, l, seq, tr, 2, 3, 4).reshape(bsz * t, d)
        hid = _swiglu_up(h, ffn_w1[l].astype(BF16), ffn_w3[l].astype(BF16))
        y = _mm_ksplit(hid, ffn_w2[l].astype(BF16))
        xs = _resid(xs, y, norm_g, mods, l, seq, tr, 3, 5)
    return xs[:, :seq]
```

```python
import functools
import math

import numpy as np
import jax
import jax.numpy as jnp
from jax import lax
from jax.experimental import pallas as pl
from jax.experimental.pallas import tpu as pltpu

F32 = jnp.float32
BF16 = jnp.bfloat16
HIGHEST = lax.Precision.HIGHEST

NORM_EPS = 1e-6
N_MOD = 6
GRID_W = 64
HY_BANDS = 16
HY_TARGET = 1e-2
HY_FAST_PCT = 0.3
HY_SLOW_PCT = 1.5
RW_HEAD = 64
RW_DECAY_SCALE = math.exp(-0.5)
RW_GN_EPS = 64e-5
HG_HEAD = 128
HG_CHUNK = 64
AT_HEAD = 128
AT_KV_HEADS = 4
ROPE_THETA = 10000.0

LANES = 128
SUBLANES = 8
VMEM_LIMIT = 56 << 20


def _params(*sem):
    return pltpu.CompilerParams(dimension_semantics=sem, vmem_limit_bytes=VMEM_LIMIT)


def _dot(a, b):
    return jnp.dot(a.astype(BF16), b.astype(BF16), preferred_element_type=F32)


def _dot_f32(a, b):
    return jnp.dot(a, b, precision=HIGHEST, preferred_element_type=F32)


def _dot_nt(a, b, exact=False):
    dn = (((1,), (1,)), ((), ()))
    if exact:
        return lax.dot_general(a, b, dn, precision=HIGHEST, preferred_element_type=F32)
    return lax.dot_general(a.astype(BF16), b.astype(BF16), dn, preferred_element_type=F32)


def _dot_tn(a, b, exact=False):
    dn = (((0,), (0,)), ((), ()))
    if exact:
        return lax.dot_general(a, b, dn, precision=HIGHEST, preferred_element_type=F32)
    return lax.dot_general(a.astype(BF16), b.astype(BF16), dn, preferred_element_type=F32)


def _sigmoid(x):
    return jax.nn.sigmoid(x)


def _pick(n, prefs):
    for p in prefs:
        if n % p == 0:
            return p
    return n


def _adaln_kernel(cond_ref, down_ref, up_ref, bias_ref, o_ref):
    cnd = cond_ref[...]
    t = _dot_f32(cnd * _sigmoid(cnd), down_ref[...])
    o_ref[...] = _dot_f32(t, up_ref[...]) + bias_ref[...]


def _adaln(cond8, down, up, bias):
    depth, d, r = down.shape
    n = up.shape[2]
    tn = _pick(n, (2048, 1024, 512, 256, 128))
    return pl.pallas_call(
        _adaln_kernel,
        grid=(depth, n // tn),
        in_specs=[
            pl.BlockSpec((SUBLANES, d), lambda l, j: (0, 0)),
            pl.BlockSpec((None, d, r), lambda l, j: (l, 0, 0)),
            pl.BlockSpec((None, r, tn), lambda l, j: (l, 0, j)),
            pl.BlockSpec((None, 1, tn), lambda l, j: (l, 0, j)),
        ],
        out_specs=pl.BlockSpec((None, SUBLANES, tn), lambda l, j: (l, 0, j)),
        out_shape=jax.ShapeDtypeStruct((depth, SUBLANES, n), F32),
        compiler_params=_params("arbitrary", "arbitrary"),
        name="adaln",
    )(cond8, down, up, bias.reshape(depth, 1, n))


def _rms(x):
    return x * lax.rsqrt(jnp.mean(x * x, axis=-1, keepdims=True) + NORM_EPS)


def _norm_mod_kernel(x_ref, g_ref, mod_ref, o_ref, *, g_row, shift_row, scale_row):
    y = _rms(x_ref[...]) * g_ref[g_row:g_row + 1, :]
    y = y * (1.0 + mod_ref[scale_row:scale_row + 1, :]) + mod_ref[shift_row:shift_row + 1, :]
    o_ref[...] = y.astype(o_ref.dtype)


def _resid_kernel(x_ref, y_ref, g_ref, mod_ref, o_ref, *, g_row, gate_row):
    yn = _rms(y_ref[...]) * g_ref[g_row:g_row + 1, :]
    o_ref[...] = x_ref[...] + mod_ref[gate_row:gate_row + 1, :] * yn


def _row_specs(seq, tr, d, layer):
    n_lat = seq // tr
    tok = pl.BlockSpec((None, tr, d), lambda b, i: (b, i, 0))
    g = pl.BlockSpec((None, 4, d), lambda b, i: (layer, 0, 0))
    mod = pl.BlockSpec((None, None, None, N_MOD, d),
                       lambda b, i: (layer, b, jnp.where(i < n_lat, 0, 1), 0, 0))
    return tok, g, mod


def _norm_mod(xs, norm_g, mods, layer, seq, tr, g_row, shift_row, scale_row):
    b, t, d = xs.shape
    tok, g, mod = _row_specs(seq, tr, d, layer)
    return pl.pallas_call(
        functools.partial(_norm_mod_kernel, g_row=g_row, shift_row=shift_row, scale_row=scale_row),
        grid=(b, t // tr),
        in_specs=[tok, g, mod],
        out_specs=tok,
        out_shape=jax.ShapeDtypeStruct((b, t, d), BF16),
        compiler_params=_params("parallel", "parallel"),
        name="norm_mod",
    )(xs, norm_g, mods)


def _resid(xs, y, norm_g, mods, layer, seq, tr, g_row, gate_row):
    b, t, d = xs.shape
    tok, g, mod = _row_specs(seq, tr, d, layer)
    return pl.pallas_call(
        functools.partial(_resid_kernel, g_row=g_row, gate_row=gate_row),
        grid=(b, t // tr),
        in_specs=[tok, tok, g, mod],
        out_specs=tok,
        out_shape=jax.ShapeDtypeStruct((b, t, d), F32),
        compiler_params=_params("parallel", "parallel"),
        name="resid",
    )(xs, y.reshape(b, t, d), norm_g, mods)


def _mm_kernel(*refs, n_in):
    a_refs, w_refs, o_ref = refs[:n_in], refs[n_in:2 * n_in], refs[2 * n_in]
    acc = _dot(a_refs[0][...], w_refs[0][...])
    for a_ref, w_ref in zip(a_refs[1:], w_refs[1:]):
        acc = acc + _dot(a_ref[...], w_ref[...])
    o_ref[...] = acc.astype(o_ref.dtype)


def _mm(a_list, w, col0, ncols, out_dtype, tm=512, tn=512):
    m = a_list[0].shape[0]
    tm = _pick(m, (tm, 256, 128))
    tn = next(t for t in (tn, 256, 128) if ncols % t == 0 and col0 % t == 0)
    n_in = len(a_list)
    kw = a_list[0].shape[1]
    a_specs = [pl.BlockSpec((tm, kw), lambda i, j: (i, 0)) for _ in a_list]
    w_specs = [pl.BlockSpec((kw, tn), functools.partial(lambda i, j, r: (r, col0 // tn + j), r=r))
               for r in range(n_in)]
    return pl.pallas_call(
        functools.partial(_mm_kernel, n_in=n_in),
        grid=(m // tm, ncols // tn),
        in_specs=a_specs + w_specs,
        out_specs=pl.BlockSpec((tm, tn), lambda i, j: (i, j)),
        out_shape=jax.ShapeDtypeStruct((m, ncols), out_dtype),
        compiler_params=_params("parallel", "parallel"),
        name="proj",
    )(*a_list, *([w] * n_in))


def _swiglu_kernel(a_ref, w1_ref, w3_ref, o_ref):
    a = a_ref[...]
    g = _dot(a, w1_ref[...])
    u = _dot(a, w3_ref[...])
    o_ref[...] = (g * _sigmoid(g) * u).astype(o_ref.dtype)


def _swiglu_up(a, w1, w3, tm=512, tn=256):
    m, k = a.shape
    n = w1.shape[1]
    tm = _pick(m, (tm, 256, 128))
    tn = _pick(n, (tn, 128))
    return pl.pallas_call(
        _swiglu_kernel,
        grid=(m // tm, n // tn),
        in_specs=[pl.BlockSpec((tm, k), lambda i, j: (i, 0)),
                  pl.BlockSpec((k, tn), lambda i, j: (0, j)),
                  pl.BlockSpec((k, tn), lambda i, j: (0, j))],
        out_specs=pl.BlockSpec((tm, tn), lambda i, j: (i, j)),
        out_shape=jax.ShapeDtypeStruct((m, n), BF16),
        compiler_params=_params("parallel", "parallel"),
        name="swiglu_up",
    )(a, w1, w3)


def _mm_acc_kernel(a_ref, w_ref, o_ref, acc_ref):
    k = pl.program_id(2)

    @pl.when(k == 0)
    def _():
        acc_ref[...] = jnp.zeros_like(acc_ref)

    acc_ref[...] += _dot(a_ref[...], w_ref[...])

    @pl.when(k == pl.num_programs(2) - 1)
    def _():
        o_ref[...] = acc_ref[...]


def _mm_ksplit(a, w, tm=512, tn=512):
    m, kdim = a.shape
    n = w.shape[1]
    tm = _pick(m, (tm, 256, 128))
    tn = _pick(n, (tn, 256, 128))
    tk = kdim // 2 if (kdim // 2) % LANES == 0 else kdim
    return pl.pallas_call(
        _mm_acc_kernel,
        grid=(m // tm, n // tn, kdim // tk),
        in_specs=[pl.BlockSpec((tm, tk), lambda i, j, k: (i, k)),
                  pl.BlockSpec((tk, tn), lambda i, j, k: (k, j))],
        out_specs=pl.BlockSpec((tm, tn), lambda i, j, k: (i, j)),
        out_shape=jax.ShapeDtypeStruct((m, n), F32),
        scratch_shapes=[pltpu.VMEM((tm, tn), F32)],
        compiler_params=_params("parallel", "parallel", "arbitrary"),
        name="ffn_down",
    )(a, w)


def _neighbours(x, seq):
    t = x.shape[0]
    row = lax.broadcasted_iota(jnp.int32, x.shape, 0)
    prev = jnp.where(row == 0, 0.0, jnp.where(row == seq, 0.0, pltpu.roll(x, 1, 0)))
    nxt = jnp.where(row == seq - 1, 0.0, jnp.where(row == t - 1, 0.0, pltpu.roll(x, t - 1, 0)))
    return prev, nxt


def _short_conv_kernel(z_ref, p_ref, o_ref, *, seq):
    z = z_ref[...]
    prev, nxt = _neighbours(z, seq)
    o_ref[...] = p_ref[0:1, :] * prev + p_ref[1:2, :] * z + p_ref[2:3, :] * nxt + p_ref[3:4, :]


def _token_shift_kernel(z_ref, p_ref, o_ref, *, seq):
    z = z_ref[...]
    prev, nxt = _neighbours(z, seq)
    o_ref[...] = z + p_ref[...] * (0.5 * (prev + nxt) - z)


def _time_mix(body, z, params, seq, name):
    b, t, cw = z.shape
    tc = LANES
    return pl.pallas_call(
        functools.partial(body, seq=seq),
        grid=(b, cw // tc),
        in_specs=[pl.BlockSpec((None, t, tc), lambda i, j: (i, 0, j)),
                  pl.BlockSpec((params.shape[0], tc), lambda i, j: (0, j))],
        out_specs=pl.BlockSpec((None, t, tc), lambda i, j: (i, 0, j)),
        out_shape=jax.ShapeDtypeStruct((b, t, cw), F32),
        compiler_params=_params("parallel", "parallel"),
        name=name,
    )(z, params)


def _hy_hidden_kernel(bands_ref, w1t_ref, w1c_ref, w1s_ref, b1_ref, w2_ref, b2_ref, freq_ref, o_ref, *, length):
    hp = o_ref.shape[1]
    pos_b = lax.broadcasted_iota(jnp.int32, (length, bands_ref.shape[1]), 0).astype(F32)
    ang = (2 * math.pi) * pos_b / length * bands_ref[...]
    t = lax.broadcasted_iota(jnp.int32, (length, hp), 0).astype(F32) / max(length - 1, 1)
    pre = (t * w1t_ref[...] + _dot_f32(jnp.cos(ang), w1c_ref[...]) + _dot_f32(-jnp.sin(ang), w1s_ref[...])
           + b1_ref[...])
    h = jnp.sin(freq_ref[0:1, :] * pre)
    o_ref[...] = jnp.sin(freq_ref[1:2, :] * (_dot_f32(h, w2_ref[...]) + b2_ref[...]))


def _hy_hidden(length, bands, w1t, w1c, w1s, b1, w2, b2, freq):
    hp = w2.shape[0]
    full = lambda a: pl.BlockSpec(a.shape, lambda i: (0,) * a.ndim)
    args = (bands, w1t, w1c, w1s, b1, w2, b2, freq)
    return pl.pallas_call(
        functools.partial(_hy_hidden_kernel, length=length),
        grid=(1,),
        in_specs=[full(a) for a in args],
        out_specs=pl.BlockSpec((length, hp), lambda i: (0, 0)),
        out_shape=jax.ShapeDtypeStruct((length, hp), F32),
        compiler_params=_params("arbitrary"),
        name="hy_hidden",
    )(*args)


def _hy_taps_kernel(h_ref, w3_ref, delta_ref, o_ref, *, length):
    t = lax.broadcasted_iota(jnp.int32, o_ref.shape, 0).astype(F32) / max(length - 1, 1)
    o_ref[...] = _dot_f32(h_ref[...], w3_ref[...]) * jnp.exp(-t * delta_ref[...])


def _hy_taps(hmid, w3p, delta, c):
    length, hp = hmid.shape
    tc = _pick(c, (256, 128))
    nct = c // tc
    return pl.pallas_call(
        functools.partial(_hy_taps_kernel, length=length),
        grid=(4, nct),
        in_specs=[pl.BlockSpec((length, hp), lambda g, j: (0, 0)),
                  pl.BlockSpec((hp, tc), lambda g, j: (0, g * nct + j)),
                  pl.BlockSpec((1, tc), lambda g, j: (0, j))],
        out_specs=pl.BlockSpec((None, length, tc), lambda g, j: (g, 0, j)),
        out_shape=jax.ShapeDtypeStruct((4, length, c), F32),
        compiler_params=_params("parallel", "parallel"),
        name="hy_taps",
    )(hmid, w3p, delta)


def _fft_tables(n1, n2):
    n = n1 * n2
    n1h = n1 // 2

    def cs(phase, period):
        ang = (phase % period).astype(F32) * (2 * math.pi / period)
        return jnp.cos(ang), jnp.sin(ang)

    k1 = jnp.arange(n1, dtype=jnp.int32)
    c, s = cs(k1[:, None] * jnp.arange(n1h, dtype=jnp.int32)[None, :], n1)
    f_a = jnp.concatenate([c, -s], axis=0)
    c, s = cs(jnp.arange(n1h, dtype=jnp.int32)[:, None] * k1[None, :], n1)
    f_c = jnp.concatenate([c, -s], axis=1) / n
    k2 = jnp.arange(n2, dtype=jnp.int32)
    freq = k1[:, None, None] + n1 * k2[None, :, None]
    c, s = cs(freq * k2[None, None, :], n)
    g = jnp.concatenate([jnp.concatenate([c, s], axis=2),
                         jnp.concatenate([-s, c], axis=2)], axis=1)
    ct, st = jnp.swapaxes(c, 1, 2), jnp.swapaxes(s, 1, 2)
    gh = jnp.concatenate([jnp.concatenate([ct, -st], axis=2),
                          jnp.concatenate([st, ct], axis=2)], axis=1)
    return f_a.astype(BF16), f_c.astype(BF16), g.astype(BF16), gh.astype(BF16)


def _fft_a_kernel(u_ref, f_ref, ar_ref, ai_ref):
    n1 = ar_ref.shape[0]
    res = _dot(f_ref[...], u_ref[...])
    ar_ref[...] = res[:n1].astype(BF16)
    ai_ref[...] = res[n1:].astype(BF16)


def _merged(groups, g):
    return lambda rows, c: pl.BlockSpec((None, rows, c), lambda b, j: (b, 0, j * groups + g))


def _split(g):
    return lambda rows, c: pl.BlockSpec((None, None, rows, c), lambda b, j: (b, j, 0, g))


def _fft_a(src, layout, c, n2, f_a):
    bn = src.shape[0]
    n1 = f_a.shape[0] // 2
    out = jax.ShapeDtypeStruct((bn, n1, n2 * c), BF16)
    ospec = pl.BlockSpec((None, n1, c), lambda b, j: (b, 0, j))
    ar, ai = pl.pallas_call(
        _fft_a_kernel,
        grid=(bn, n2),
        in_specs=[layout(n1 // 2, c), pl.BlockSpec(f_a.shape, lambda b, j: (0, 0))],
        out_specs=[ospec, ospec],
        out_shape=[out, out],
        compiler_params=_params("parallel", "parallel"),
        name="fft_a",
    )(src, f_a)
    return ar.reshape(bn, n1, n2, c), ai.reshape(bn, n1, n2, c)


def _stack(r_ref, i_ref):
    return jnp.concatenate([r_ref[...], i_ref[...]], axis=0)


def _fft_spec_kernel(fr_ref, fi_ref, br_ref, bi_ref, g_ref, kr_ref, ki_ref):
    n2 = kr_ref.shape[0]
    xf = jnp.dot(g_ref[...], _stack(fr_ref, fi_ref), preferred_element_type=F32)
    xb = jnp.dot(g_ref[...], _stack(br_ref, bi_ref), preferred_element_type=F32)
    kr_ref[...] = xf[:n2] + xb[:n2]
    ki_ref[...] = xf[n2:] - xb[n2:]


def _fft_spec(tr, ti, g):
    _, n1, n2, c = tr.shape
    fwd = pl.BlockSpec((None, None, n2, c), lambda o, k: (o, k, 0, 0))
    bwd = pl.BlockSpec((None, None, n2, c), lambda o, k: (2 + o, k, 0, 0))
    out = jax.ShapeDtypeStruct((2, n1, n2, c), F32)
    return pl.pallas_call(
        _fft_spec_kernel,
        grid=(2, n1),
        in_specs=[fwd, fwd, bwd, bwd, pl.BlockSpec((None, 2 * n2, 2 * n2), lambda o, k: (k, 0, 0))],
        out_specs=[fwd, fwd],
        out_shape=[out, out],
        compiler_params=_params("parallel", "parallel"),
        name="fft_spec",
    )(tr, ti, tr, ti, g)


def _fft_mid_kernel(ar_ref, ai_ref, g_ref, gh_ref, kr_ref, ki_ref, zr_ref, zi_ref):
    n2 = zr_ref.shape[0]
    x = jnp.dot(g_ref[...], _stack(ar_ref, ai_ref), preferred_element_type=F32)
    xr, xi = x[:n2], x[n2:]
    kr, ki = kr_ref[...], ki_ref[...]
    y = jnp.concatenate([(xr * kr - xi * ki).astype(BF16), (xr * ki + xi * kr).astype(BF16)], axis=0)
    z = jnp.dot(gh_ref[...], y, preferred_element_type=F32)
    zr_ref[...] = z[:n2].astype(BF16)
    zi_ref[...] = z[n2:].astype(BF16)


def _fft_mid(ar, ai, g, gh, kr, ki, order):
    bn, n1, n2, c = ar.shape
    act = pl.BlockSpec((None, None, n2, c), lambda k, b: (b, k, 0, 0))
    mat = pl.BlockSpec((None, 2 * n2, 2 * n2), lambda k, b: (k, 0, 0))
    spec = pl.BlockSpec((None, None, n2, c), lambda k, b: (order, k, 0, 0))
    out = jax.ShapeDtypeStruct((bn, n1, n2, c), BF16)
    return pl.pallas_call(
        _fft_mid_kernel,
        grid=(n1, bn),
        in_specs=[act, act, mat, mat, spec, spec],
        out_specs=[act, act],
        out_shape=[out, out],
        compiler_params=_params("parallel", "parallel"),
        name="fft_mid",
    )(ar, ai, g, gh, kr, ki)


def _fft_c_kernel(zr_ref, zi_ref, f_ref, u_ref, gate_ref, bias_ref, o_ref):
    conv = jnp.dot(f_ref[...], _stack(zr_ref, zi_ref), preferred_element_type=F32)
    u = u_ref[...]
    o_ref[...] = gate_ref[...] * (conv + bias_ref[...] * u)


def _fft_c(zr, zi, f_c, u, u_layout, gate, gate_layout, bias, order):
    bn, n1, n2, c = zr.shape
    zspec = pl.BlockSpec((None, n1, c), lambda b, j: (b, 0, j))
    return pl.pallas_call(
        _fft_c_kernel,
        grid=(bn, n2),
        in_specs=[zspec, zspec, pl.BlockSpec(f_c.shape, lambda b, j: (0, 0)), u_layout(n1 // 2, c),
                  gate_layout(n1 // 2, c), pl.BlockSpec((None, 1, c), lambda b, j: (order, 0, 0))],
        out_specs=_merged(1, 0)(n1 // 2, c),
        out_shape=jax.ShapeDtypeStruct((bn, n1 // 2, n2 * c), F32),
        compiler_params=_params("parallel", "parallel"),
        name="fft_c",
    )(zr.reshape(bn, n1, n2 * c), zi.reshape(bn, n1, n2 * c), f_c, u, gate, bias)


def _hy_ctx_kernel(x1_ref, x2_ref, v_ref, taps_ref, bias_ref, fwd_ref, inv_ref, o_ref):
    n = fwd_ref.shape[0] // 2
    fwd, inv = fwd_ref[...], inv_ref[...]

    def spec(a):
        s = _dot_f32(fwd, a)
        return s[:n], s[n:]

    u = v_ref[...]
    for o, gate_ref in ((0, x1_ref), (1, x2_ref)):
        fr, fi = spec(taps_ref[o])
        br, bi = spec(taps_ref[2 + o])
        kr, ki = fr + br, fi - bi
        ur, ui = spec(u)
        y = jnp.concatenate([ur * kr - ui * ki, ur * ki + ui * kr], axis=0)
        u = gate_ref[...] * (_dot_f32(inv, y) + bias_ref[o:o + 1, :] * u)
    o_ref[...] = u


def _hy_ctx(zc, taps, bias, seq, ctx_len, c):
    b = zc.shape[0]
    n = 2 * ctx_len
    tc = _pick(c, (256, 128))
    nct = c // tc
    k = jnp.arange(n, dtype=jnp.int32)[:, None]
    m = jnp.arange(ctx_len, dtype=jnp.int32)[None, :]
    ang = ((k * m) % n).astype(F32) * (2 * math.pi / n)
    fwd = jnp.concatenate([jnp.cos(ang), -jnp.sin(ang)], axis=0)
    inv = jnp.concatenate([jnp.cos(ang).T, -jnp.sin(ang).T], axis=1) / n
    row = seq // ctx_len
    tok = lambda g: pl.BlockSpec((None, ctx_len, tc), lambda j, i: (i, row, g * nct + j))
    return pl.pallas_call(
        _hy_ctx_kernel,
        grid=(nct, b),
        in_specs=[tok(0), tok(1), tok(2),
                  pl.BlockSpec((4, ctx_len, tc), lambda j, i: (0, 0, j)),
                  pl.BlockSpec((2, tc), lambda j, i: (0, j)),
                  pl.BlockSpec(fwd.shape, lambda j, i: (0, 0)),
                  pl.BlockSpec(inv.shape, lambda j, i: (0, 0))],
        out_specs=pl.BlockSpec((None, ctx_len, tc), lambda j, i: (i, 0, j)),
        out_shape=jax.ShapeDtypeStruct((b, ctx_len, c), F32),
        compiler_params=_params("parallel", "parallel"),
        name="hy_ctx",
    )(zc, zc, zc, taps, bias, fwd, inv)


def _hyena(zh, hy, seq, ctx_len, tables):
    short_w, short_b, w1, b1, w2, b2, freq, w3, bias = hy
    b, t, c3 = zh.shape
    c = c3 // 3
    f_a, f_c, g, gh = tables
    n2 = LANES
    zc = _time_mix(_short_conv_kernel, zh, jnp.concatenate([short_w, short_b[None]], axis=0), seq, "short_conv")

    hf = w2.shape[0]
    hp = LANES
    pad_c = lambda a: jnp.pad(a, ((0, 0), (0, hp - hf)))
    bands = jnp.pad(jnp.linspace(1e-4, HY_BANDS - 1, HY_BANDS, dtype=F32)[None, :], ((0, 0), (0, LANES - HY_BANDS)))
    w1c = jnp.pad(w1[1:1 + HY_BANDS], ((0, LANES - HY_BANDS), (0, hp - hf)))
    w1s = jnp.pad(w1[1 + HY_BANDS:], ((0, LANES - HY_BANDS), (0, hp - hf)))
    hid_args = (bands, pad_c(w1[0:1]), w1c, w1s, pad_c(b1[None]), jnp.pad(w2, ((0, hp - hf), (0, hp - hf))),
                pad_c(b2[None]), pad_c(freq))
    w3p = jnp.pad(w3, ((0, hp - hf), (0, 0)))
    max_decay = math.log(HY_TARGET) / HY_FAST_PCT
    min_decay = math.log(HY_TARGET) / HY_SLOW_PCT
    delta = jnp.abs(jnp.linspace(min_decay, max_decay, c, dtype=F32))[None, :]

    taps_l = _hy_taps(_hy_hidden(seq, *hid_args), w3p, delta, c)
    taps_c = _hy_taps(_hy_hidden(ctx_len, *hid_args), w3p, delta, c)

    as_split = lambda a: jnp.swapaxes(a.reshape(a.shape[0], seq // n2, n2, a.shape[-1]), 1, 2)
    tr, ti = _fft_a(as_split(taps_l), _split(0), c, n2, f_a)
    kr, ki = _fft_spec(tr, ti, g)

    zct = as_split(zc[:, :seq])
    bias3 = bias.reshape(2, 1, c)
    ar, ai = _fft_a(zct, _split(2), c, n2, f_a)
    zr, zi = _fft_mid(ar, ai, g, gh, kr, ki, 0)
    u1 = _fft_c(zr, zi, f_c, zct, _split(2), zct, _split(0), bias3, 0)
    ar, ai = _fft_a(u1, _merged(1, 0), c, n2, f_a)
    zr, zi = _fft_mid(ar, ai, g, gh, kr, ki, 1)
    u2 = _fft_c(zr, zi, f_c, u1, _merged(1, 0), zct, _split(1), bias3, 1)
    out_c = _hy_ctx(zc, taps_c, bias, seq, ctx_len, c)
    return jnp.concatenate([u2.reshape(b, seq, c), out_c], axis=1)


def _head_sum(x, n_heads):
    fold = x[:, :LANES]
    for s in range(LANES, x.shape[1], LANES):
        fold = fold + x[:, s:s + LANES]
    r = lax.broadcasted_iota(jnp.int32, (LANES, LANES), 0) & (n_heads - 1)
    c = lax.broadcasted_iota(jnp.int32, (LANES, LANES), 1) & (n_heads - 1)
    per_head = _dot_f32(fold, jnp.where(r == c, 1.0, 0.0))
    return jnp.concatenate([per_head] * (x.shape[1] // LANES), axis=1)


def _rw_feat_kernel(r_ref, k_ref, v_ref, zl_ref, wup_ref, aup_ref, gup_ref, w0_ref, a0_ref, kkp_ref, kap_ref,
                    rk_ref, kk_o, g_o, bonus_o, w_o, kd_o, kka_o, *, n_heads):
    r, k, v = r_ref[...], k_ref[...], v_ref[...]
    zl = zl_ref[...]
    wl, al, gl = zl[:, :LANES], zl[:, LANES:2 * LANES], zl[:, 2 * LANES:]
    kq = k * kkp_ref[...]
    kk = kq * lax.rsqrt(_head_sum(kq * kq, n_heads) + 1e-12)
    kk_o[...] = kk
    g_o[...] = _dot(_sigmoid(gl), gup_ref[...])
    twl = jnp.tanh(wl)
    bonus = jnp.zeros_like(v)
    for d in range(2):
        w = jnp.exp(-RW_DECAY_SCALE * _sigmoid(w0_ref[d:d + 1, :] + _dot_f32(twl, wup_ref[d])))
        a = _sigmoid(a0_ref[d:d + 1, :] + _dot_f32(al, aup_ref[d]))
        kd = k * (1.0 + (a - 1.0) * kap_ref[...])
        w_o[d] = w
        kd_o[d] = kd
        kka_o[d] = kk * a
        bonus = bonus + _head_sum(r * kd * rk_ref[...], n_heads) * v
    bonus_o[...] = bonus


def _rw_features(zs, zl, w_up, a_up, g_up, w0, a0, k_k, k_a, r_k, tr):
    b, t, c3 = zs.shape
    c = c3 // 3
    tok = lambda g: pl.BlockSpec((None, tr, c), lambda i, j: (i, j, g))
    full = lambda a: pl.BlockSpec(a.shape, lambda i, j: (0,) * a.ndim)
    two = pl.BlockSpec((2, None, tr, c), lambda i, j: (0, i, j, 0))
    s1 = jax.ShapeDtypeStruct((b, t, c), F32)
    s2 = jax.ShapeDtypeStruct((2, b, t, c), F32)
    consts = (w_up, a_up, g_up, w0, a0, k_k, k_a, r_k)
    return pl.pallas_call(
        functools.partial(_rw_feat_kernel, n_heads=c // RW_HEAD),
        grid=(b, t // tr),
        in_specs=[tok(0), tok(1), tok(2), pl.BlockSpec((None, tr, zl.shape[2]), lambda i, j: (i, j, 0))]
        + [full(a) for a in consts],
        out_specs=[tok(0), tok(0), tok(0), two, two, two],
        out_shape=[s1, s1, s1, s2, s2, s2],
        compiler_params=_params("parallel", "parallel"),
        name="rw_features",
    )(zs, zs, zs, zl, *consts)


RW_STEPS = 32
RW_PARTIALS = 4


def _rw_scan_kernel(r_ref, w_ref, kd_ref, v_ref, kk_ref, kka_ref, y_ref, s_ref):
    hd = s_ref.shape[0]

    @pl.when(pl.program_id(0) == 0)
    def _():
        s_ref[...] = jnp.zeros_like(s_ref)

    def tree(parts):
        while len(parts) > 1:
            parts = [a + b for a, b in zip(parts[::2], parts[1::2])]
        return parts[0]

    def step(i, carry):
        row = lambda ref, k: ref[i, k:k + 1, :]
        parts = [None] * RW_PARTIALS
        for k in range(hd):
            term = s_ref[k] * row(kk_ref, k)
            parts[k % RW_PARTIALS] = term if parts[k % RW_PARTIALS] is None else parts[k % RW_PARTIALS] + term
        removed = tree(parts)
        v = v_ref[i]
        parts = [None] * RW_PARTIALS
        for k in range(hd):
            s = s_ref[k] * row(w_ref, k) - removed * row(kka_ref, k) + v * row(kd_ref, k)
            s_ref[k] = s
            term = s * row(r_ref, k)
            parts[k % RW_PARTIALS] = term if parts[k % RW_PARTIALS] is None else parts[k % RW_PARTIALS] + term
        y_ref[i] = tree(parts)
        return carry

    lax.fori_loop(0, y_ref.shape[0], step, 0)


def _segment_order(n_lat, n_ctx, reverse):
    if reverse:
        return lambda j: n_lat + n_ctx - 1 - j
    return lambda j: jnp.where(j < n_ctx, n_lat + j, j - n_ctx)


def _rw_scan(r, w, kd, v, kk, kka):
    t, hd, lc = r.shape
    blk = pl.BlockSpec((RW_STEPS, hd, lc), lambda j: (j, 0, 0))
    return pl.pallas_call(
        _rw_scan_kernel,
        grid=(t // RW_STEPS,),
        in_specs=[blk] * 6,
        out_specs=blk,
        out_shape=jax.ShapeDtypeStruct((t, hd, lc), F32),
        scratch_shapes=[pltpu.VMEM((hd, hd, lc), F32)],
        compiler_params=_params("arbitrary"),
        name="rw_scan",
    )(r, w, kd, v, kk, kka)


def _rw_out_kernel(y0_ref, y1_ref, bonus_ref, g_ref, lnw_ref, lnb_ref, o_ref, *, n_heads):
    y = y0_ref[...] + y1_ref[...]
    mean = _head_sum(y, n_heads) * (1.0 / RW_HEAD)
    yc = y - mean
    var = _head_sum(yc * yc, n_heads) * (1.0 / RW_HEAD)
    yn = yc * lax.rsqrt(var + RW_GN_EPS) * lnw_ref[...] + lnb_ref[...]
    o_ref[...] = ((yn + bonus_ref[...]) * g_ref[...]).astype(o_ref.dtype)


def _rw_out(y0, y1, bonus, g, ln_w, ln_b, tr):
    b, t, c = y0.shape
    tok = pl.BlockSpec((None, tr, c), lambda i, j: (i, j, 0))
    vec = pl.BlockSpec((1, c), lambda i, j: (0, 0))
    return pl.pallas_call(
        functools.partial(_rw_out_kernel, n_heads=c // RW_HEAD),
        grid=(b, t // tr),
        in_specs=[tok, tok, tok, tok, vec, vec],
        out_specs=tok,
        out_shape=jax.ShapeDtypeStruct((b, t, c), BF16),
        compiler_params=_params("parallel", "parallel"),
        name="rw_out",
    )(y0, y1, bonus, g, ln_w, ln_b)


def _pad_rows(a, rows):
    return jnp.pad(a, [(0, 0)] * (a.ndim - 2) + [(0, rows - a.shape[-2]), (0, 0)])


def _rw_perm(c):
    return jnp.arange(c).reshape(c // RW_HEAD, RW_HEAD).T.reshape(-1)


def _rwkv(zs_raw, zl_raw, rw, seq, tr):
    mu, w0, w_up, a0, a_up, g_up, k_k, k_a, r_k, ln_w, ln_b = rw
    b, t, _ = zs_raw.shape
    c = k_k.shape[0]
    n_heads = c // RW_HEAD
    perm = _rw_perm(c)
    pc = lambda a: jnp.take(a, perm, axis=-1)
    lw, la = w_up.shape[1], a_up.shape[1]
    mu_l = mu[3 * c:]
    pad_to = lambda a: jnp.pad(a, (0, LANES - a.shape[0]))
    mu_lora = jnp.concatenate([pad_to(mu_l[:lw]), pad_to(mu_l[lw:lw + la]), mu_l[lw + la:]])
    mu_main = pc(mu[:3 * c].reshape(3, c)).reshape(1, 3 * c)
    zs = _time_mix(_token_shift_kernel, zs_raw, mu_main, seq, "token_shift")
    zl = _time_mix(_token_shift_kernel, zl_raw, mu_lora[None, :], seq, "token_shift")
    kk, g, bonus, w, kd, kka = _rw_features(
        zs, zl, _pad_rows(pc(w_up), LANES), _pad_rows(pc(a_up), LANES), pc(g_up).astype(BF16), pc(w0), pc(a0),
        pc(k_k)[None], pc(k_a)[None], pc(r_k.reshape(c))[None], min(tr, 128))

    def chains(fwd, bwd):
        f = fwd.reshape(b, t, RW_HEAD, n_heads)
        f = jnp.concatenate([f[:, seq:], f[:, :seq]], axis=1)
        r = bwd.reshape(b, t, RW_HEAD, n_heads)[:, ::-1]
        return jnp.stack([f, r], axis=0).transpose(2, 3, 0, 1, 4).reshape(t, RW_HEAD, 2 * b * n_heads)

    r3, v3 = zs[..., :c], zs[..., 2 * c:]
    y = _rw_scan(chains(r3, r3), chains(w[0], w[1]), chains(kd[0], kd[1]), chains(v3, v3), chains(kk, kk),
                 chains(kka[0], kka[1]))
    y = y.reshape(t, RW_HEAD, 2, b, n_heads).transpose(2, 3, 0, 1, 4).reshape(2, b, t, c)
    y0 = jnp.concatenate([y[0][:, t - seq:], y[0][:, :t - seq]], axis=1)
    y1 = y[1][:, ::-1]
    return _rw_out(y0, y1, bonus, g, pc(ln_w)[None], pc(ln_b)[None], tr)


def _hg_bounds_kernel(x_ref, o_ref):
    x = x_ref[...]
    e = jnp.exp(x - jnp.max(x, axis=0, keepdims=True))
    p = e / jnp.sum(e, axis=0, keepdims=True)
    run = jnp.zeros_like(p[0:1])
    o_ref[0:1, :] = run
    for i in range(1, x.shape[0]):
        run = run + p[i:i + 1]
        o_ref[i:i + 1, :] = run


def _hg_bounds(lower):
    return pl.pallas_call(
        _hg_bounds_kernel,
        out_shape=jax.ShapeDtypeStruct(lower.shape, F32),
        name="hg_bounds",
    )(lower)


HG_HEADS_PER_STEP = 16


def _hg_scan_kernel(q_ref, f_ref, i_ref, lb_ref, o_ref, s_ref, *, reverse):
    cs = q_ref.shape[0]
    n_heads = s_ref.shape[0]

    @pl.when(pl.program_id(2) == 0)
    def _():
        s_ref[...] = jnp.zeros_like(s_ref)

    width = q_ref.shape[1]
    row = lax.broadcasted_iota(jnp.int32, (cs, cs), 0)
    col = lax.broadcasted_iota(jnp.int32, (cs, cs), 1)
    row_w = lax.broadcasted_iota(jnp.int32, (cs, width), 0)
    inclusive = lambda upto: jnp.where((col >= upto) if reverse else (col <= upto), 1.0, 0.0)
    sums = [inclusive(row)]
    levels = []
    hs = cs // 2
    while hs >= 1:
        blk = 2 * hs
        boundary = (row & ~(blk - 1)) + hs
        sums.append(inclusive(boundary if reverse else boundary - 1))
        levels.append(((row & ~(blk - 1)) == (col & ~(blk - 1)), (row_w & (blk - 1)) >= hs))
        hs //= 2

    heads = [slice(h * HG_HEAD, (h + 1) * HG_HEAD) for h in range(n_heads)]
    q, v, lb = q_ref[...], i_ref[...], lb_ref[...]
    f = lb + (1.0 - lb) * _sigmoid(f_ref[...])
    k = 1.0 - f
    bb = _dot_f32(jnp.concatenate(sums, axis=0), jnp.log(f))
    b = bb[:cs]
    qk = q * k
    att = [jnp.where(row == col, jnp.sum(qk[:, sl], axis=-1, keepdims=True), 0.0) for sl in heads]
    for l, (same, late) in enumerate(levels):
        ref_b = bb[(l + 1) * cs:(l + 2) * cs]
        q_side, k_side = (jnp.logical_not(late), late) if reverse else (late, jnp.logical_not(late))
        qt = (q * jnp.exp(jnp.where(q_side, b - ref_b, -jnp.inf))).astype(BF16)
        kt = (k * jnp.exp(jnp.where(k_side, ref_b - b, -jnp.inf))).astype(BF16)
        for h, sl in enumerate(heads):
            att[h] = att[h] + jnp.where(same, _dot_nt(qt[:, sl], kt[:, sl]), 0.0)
    b_end = b[0:1, :] if reverse else b[cs - 1:cs, :]
    qe = (q * jnp.exp(b)).astype(BF16)
    ke = (k * jnp.exp(b_end - b)).astype(BF16)
    decay = jnp.exp(b_end)
    vb = v.astype(BF16)
    for h, sl in enumerate(heads):
        st = s_ref[h]
        o_ref[:, sl] = _dot(att[h], vb[:, sl]) + _dot_nt(qe[:, sl], st)
        s_ref[h] = decay[:, sl] * st + _dot_tn(vb[:, sl], ke[:, sl])


def _hg_scan(p, lb, c, seq, direction):
    b, t, _ = p.shape
    wl = min(HG_HEADS_PER_STEP * HG_HEAD, c)
    ng = c // wl
    reverse = direction == 1
    order = _segment_order(seq // HG_CHUNK, (t - seq) // HG_CHUNK, reverse)
    blk = lambda g: pl.BlockSpec((None, HG_CHUNK, wl), lambda i, n, j: (i, order(j), g * ng + n))
    return pl.pallas_call(
        functools.partial(_hg_scan_kernel, reverse=reverse),
        grid=(b, ng, t // HG_CHUNK),
        in_specs=[blk(0), blk(1 + direction), blk(3), pl.BlockSpec((1, wl), lambda i, n, j: (0, n))],
        out_specs=blk(0),
        out_shape=jax.ShapeDtypeStruct((b, t, c), F32),
        scratch_shapes=[pltpu.VMEM((wl // HG_HEAD, HG_HEAD, HG_HEAD), F32)],
        compiler_params=_params("parallel", "parallel", "arbitrary"),
        name="hg_scan",
    )(p, p, p, lb)


def _hg_out_kernel(o0_ref, o1_ref, g_ref, w_ref, out_ref):
    o = o0_ref[...] + o1_ref[...]
    g = g_ref[...]
    for s in range(0, o.shape[1], HG_HEAD):
        oh = o[:, s:s + HG_HEAD]
        gh = g[:, s:s + HG_HEAD]
        out_ref[:, s:s + HG_HEAD] = (_rms(oh) * w_ref[...] * (gh * _sigmoid(gh))).astype(out_ref.dtype)


def _hg_out(o0, o1, p, norm_w, tr):
    b, t, c = o0.shape
    tc = _pick(c, (512, 256, 128))
    nct = c // tc
    tok = pl.BlockSpec((None, tr, tc), lambda i, j, n: (i, j, n))
    return pl.pallas_call(
        _hg_out_kernel,
        grid=(b, t // tr, nct),
        in_specs=[tok, tok, pl.BlockSpec((None, tr, tc), lambda i, j, n: (i, j, 4 * nct + n)),
                  pl.BlockSpec((1, HG_HEAD), lambda i, j, n: (0, 0))],
        out_specs=tok,
        out_shape=jax.ShapeDtypeStruct((b, t, c), BF16),
        compiler_params=_params("parallel", "parallel", "parallel"),
        name="hg_out",
    )(o0, o1, p, norm_w)


def _rope_tables(seq):
    t = jnp.arange(seq)
    axis = AT_HEAD // 2
    inv = ROPE_THETA ** (-jnp.arange(0, axis, 2, dtype=F32) / axis)
    ang = jnp.concatenate([(t // GRID_W)[:, None] * inv, (t % GRID_W)[:, None] * inv], axis=-1)
    sign = jnp.tile(jnp.array([-1.0, 1.0], F32), AT_HEAD // 2)
    return jnp.repeat(jnp.cos(ang), 2, axis=-1), jnp.repeat(jnp.sin(ang), 2, axis=-1) * sign


def _at_prep_kernel(q_ref, k_ref, qw_ref, kw_ref, cos_ref, sin_ref, qo_ref, ko_ref, *, n_lat):
    is_lat = pl.program_id(1) < n_lat
    cos, sin = cos_ref[...], sin_ref[...]
    even = (lax.broadcasted_iota(jnp.int32, cos.shape, 1) & 1) == 0

    def head(x, w):
        y = _rms(x) * w
        partner = jnp.where(even, pltpu.roll(y, AT_HEAD - 1, 1), pltpu.roll(y, 1, 1))
        return jnp.where(is_lat, y * cos + partner * sin, y)

    scale = AT_HEAD ** -0.5
    for s in range(0, qo_ref.shape[1], AT_HEAD):
        qo_ref[:, s:s + AT_HEAD] = (head(q_ref[:, s:s + AT_HEAD], qw_ref[...]) * scale).astype(qo_ref.dtype)
    for s in range(0, ko_ref.shape[1], AT_HEAD):
        ko_ref[:, s:s + AT_HEAD] = head(k_ref[:, s:s + AT_HEAD], kw_ref[...]).astype(ko_ref.dtype)


def _at_prep(p, c, q_norm, k_norm, cos, sin, seq, tr):
    b, t, _ = p.shape
    kvw = AT_KV_HEADS * AT_HEAD
    n_lat = seq // tr
    tab = pl.BlockSpec((tr, AT_HEAD), lambda i, j: (jnp.minimum(j, n_lat - 1), 0))
    vec = pl.BlockSpec((1, AT_HEAD), lambda i, j: (0, 0))
    return pl.pallas_call(
        functools.partial(_at_prep_kernel, n_lat=n_lat),
        grid=(b, t // tr),
        in_specs=[pl.BlockSpec((None, tr, c), lambda i, j: (i, j, 5)),
                  pl.BlockSpec((None, tr, kvw), lambda i, j: (i, j, 6 * c // kvw)),
                  vec, vec, tab, tab],
        out_specs=[pl.BlockSpec((None, tr, c), lambda i, j: (i, j, 0)),
                   pl.BlockSpec((None, tr, kvw), lambda i, j: (i, j, 0))],
        out_shape=[jax.ShapeDtypeStruct((b, t, c), BF16), jax.ShapeDtypeStruct((b, t, kvw), BF16)],
        compiler_params=_params("parallel", "parallel"),
        name="at_prep",
    )(p, p, q_norm, k_norm, cos, sin)


def _flash_kernel(q_ref, k_ref, v_ref, o_ref, m_ref, l_ref, acc_ref):
    kv = pl.program_id(3)

    @pl.when(kv == 0)
    def _():
        m_ref[...] = jnp.full_like(m_ref, -jnp.inf)
        l_ref[...] = jnp.zeros_like(l_ref)
        acc_ref[...] = jnp.zeros_like(acc_ref)

    k = k_ref[...]
    v = v_ref[...].astype(BF16)
    group = m_ref.shape[0]
    for r in range(group):
        s = _dot_nt(q_ref[:, r * AT_HEAD:(r + 1) * AT_HEAD], k)
        m_prev = m_ref[r]
        m_cur = jnp.maximum(m_prev, jnp.max(s, axis=-1, keepdims=True))
        alpha = jnp.exp(m_prev - m_cur)
        p = jnp.exp(s - m_cur[:, 0:1])
        l_ref[r] = alpha * l_ref[r] + jnp.sum(p, axis=-1, keepdims=True)
        acc_ref[r] = alpha * acc_ref[r] + _dot(p, v)
        m_ref[r] = m_cur

    @pl.when(kv == pl.num_programs(3) - 1)
    def _():
        for r in range(group):
            o_ref[:, r * AT_HEAD:(r + 1) * AT_HEAD] = (acc_ref[r] / l_ref[r]).astype(o_ref.dtype)


def _flash(qn, kn, p, c, q_blk0, tq, nq, kv_blk0, tk, nk):
    b = qn.shape[0]
    group = c // AT_HEAD // AT_KV_HEADS
    gw = group * AT_HEAD
    v_col0 = (6 * c + AT_KV_HEADS * AT_HEAD) // AT_HEAD
    return pl.pallas_call(
        _flash_kernel,
        grid=(b, AT_KV_HEADS, nq, nk),
        in_specs=[pl.BlockSpec((None, tq, gw), lambda i, g, a, n: (i, q_blk0 + a, g)),
                  pl.BlockSpec((None, tk, AT_HEAD), lambda i, g, a, n: (i, kv_blk0 + n, g)),
                  pl.BlockSpec((None, tk, AT_HEAD), lambda i, g, a, n: (i, kv_blk0 + n, v_col0 + g))],
        out_specs=pl.BlockSpec((None, tq, gw), lambda i, g, a, n: (i, a, g)),
        out_shape=jax.ShapeDtypeStruct((b, nq * tq, c), BF16),
        scratch_shapes=[pltpu.VMEM((group, tq, AT_HEAD), F32)] * 3,
        compiler_params=_params("parallel", "parallel", "parallel", "arbitrary"),
        name="flash",
    )(qn, kn, p)


def kernel(x, c, ctx, c_ctx, ada_down, ada_up, ada_bias, norm_g, ffn_w1, ffn_w3, ffn_w2, ev_w_in, ev_w_out, hy_short_w, hy_short_b, hy_pe_w1, hy_pe_b1, hy_pe_w2, hy_pe_b2, hy_sin_freq, hy_pe_w3, hy_bias, rw_mu, rw_w0, rw_w_up, rw_a0, rw_a_up, rw_g_up, rw_k_k, rw_k_a, rw_r_k, rw_ln_w, rw_ln_b, od_w_in, od_w_out, hg_lower_bounds, hg_norm_g, at_q_norm, at_k_norm):
    bsz, seq, d = x.shape
    ctx_len = ctx.shape[1]
    t = seq + ctx_len
    depth = ada_down.shape[0]
    half = d // 2
    tr = min(256, ctx_len)
    assert seq % ctx_len == 0 and ctx_len % LANES == 0 and bsz < SUBLANES

    xs = jnp.concatenate([x, ctx], axis=1)
    cond = jnp.concatenate([c, c_ctx[None], jnp.zeros((SUBLANES - bsz - 1, d), F32)], axis=0)
    m = _adaln(cond, ada_down, ada_up, ada_bias).reshape(depth, SUBLANES, N_MOD, d)
    mods = jnp.stack([m[:, :bsz], jnp.broadcast_to(m[:, bsz:bsz + 1], (depth, bsz, N_MOD, d))], axis=2)

    n1 = 2 * seq // LANES
    tables = _fft_tables(n1, LANES)
    cos, sin = _rope_tables(seq)
    lb_all = _hg_bounds(hg_lower_bounds)

    for l in range(depth):
        h = _norm_mod(xs, norm_g, mods, l, seq, tr, 0, 0, 1).reshape(bsz * t, d)
        if l % 2 == 0:
            e = l // 2
            w_in = ev_w_in[e]
            n_hy = 3 * half
            lw, la = rw_w_up.shape[2], rw_a_up.shape[2]
            w_lo = w_in[:, 2 * n_hy:]
            pad_c = lambda a: jnp.pad(a, ((0, 0), (0, LANES - a.shape[1])))
            w_lora = jnp.concatenate([pad_c(w_lo[:, :lw]), pad_c(w_lo[:, lw:lw + la]), w_lo[:, lw + la:]], axis=1)
            perm = _rw_perm(half)
            cols = jnp.concatenate([jnp.arange(n_hy)] + [n_hy + g * half + perm for g in range(3)])
            w_main = jnp.take(w_in[:, :2 * n_hy], cols, axis=1).astype(BF16)
            zh = _mm([h], w_main, 0, n_hy, F32).reshape(bsz, t, n_hy)
            zs = _mm([h], w_main, n_hy, n_hy, F32).reshape(bsz, t, n_hy)
            zl = _mm([h], w_lora.astype(BF16), 0, w_lora.shape[1], F32).reshape(bsz, t, -1)
            hy = (hy_short_w[e], hy_short_b[e], hy_pe_w1[e], hy_pe_b1[e], hy_pe_w2[e], hy_pe_b2[e],
                  hy_sin_freq[e], hy_pe_w3[e], hy_bias[e])
            rw = (rw_mu[e], rw_w0[e], rw_w_up[e], rw_a0[e], rw_a_up[e], rw_g_up[e], rw_k_k[e],
                  rw_k_a[e], rw_r_k[e], rw_ln_w[e], rw_ln_b[e])
            mix_a = _hyena(zh, hy, seq, ctx_len, tables).astype(BF16)
            mix_b = _rwkv(zs, zl, rw, seq, tr)
            w_out = jnp.take(ev_w_out[e], jnp.concatenate([jnp.arange(half), half + perm]), axis=0)
        else:
            o = l // 2
            p = _mm([h], od_w_in[o].astype(BF16), 0, od_w_in.shape[2], F32).reshape(bsz, t, -1)
            lb = lb_all[l][None, :]
            o0 = _hg_scan(p, lb, half, seq, 0)
            o1 = _hg_scan(p, lb, half, seq, 1)
            mix_a = _hg_out(o0, o1, p, hg_norm_g[o][None], tr)
            qn, kn = _at_prep(p, half, at_q_norm[o][None], at_k_norm[o][None], cos, sin, seq, tr)
            tq = _pick(seq, (512, 256, 128))
            tk = _pick(t, (768, 512, 256, 128))
            at_l = _flash(qn, kn, p, half, 0, tq, seq // tq, 0, tk, t // tk)
            at_c = _flash(qn, kn, p, half, seq // ctx_len, ctx_len, 1, seq // ctx_len, ctx_len, 1)
            mix_b = jnp.concatenate([at_l, at_c], axis=1)
            w_out = od_w_out[o]
        y = _mm([mix_a.reshape(bsz * t, half), mix_b.reshape(bsz * t, half)], w_out.astype(BF16), 0, d, F32)
        xs = _resid(xs, y, norm_g, mods, l, seq, tr, 1, 2)
        h = _norm_mod(xs, norm_g, mods, l, seq, tr, 2, 3, 4).reshape(bsz * t, d)
        hid = _swiglu_up(h, ffn_w1[l].astype(BF16), ffn_w3[l].astype(BF16))
        y = _mm_ksplit(hid, ffn_w2[l].astype(BF16))
        xs = _resid(xs, y, norm_g, mods, l, seq, tr, 3, 5)
    return xs[:, :seq]
```

```python
import functools
import math

import numpy as np
import jax
import jax.numpy as jnp
from jax import lax
from jax.experimental import pallas as pl
from jax.experimental.pallas import tpu as pltpu

F32 = jnp.float32
BF16 = jnp.bfloat16
HIGHEST = lax.Precision.HIGHEST

NORM_EPS = 1e-6
N_MOD = 6
GRID_W = 64
HY_BANDS = 16
HY_TARGET = 1e-2
HY_FAST_PCT = 0.3
HY_SLOW_PCT = 1.5
RW_HEAD = 64
RW_DECAY_SCALE = math.exp(-0.5)
RW_GN_EPS = 64e-5
HG_HEAD = 128
HG_CHUNK = 64
AT_HEAD = 128
AT_KV_HEADS = 4
ROPE_THETA = 10000.0

LANES = 128
SUBLANES = 8
VMEM_LIMIT = 56 << 20


def _params(*sem):
    return pltpu.CompilerParams(dimension_semantics=sem, vmem_limit_bytes=VMEM_LIMIT)


def _dot(a, b):
    return jnp.dot(a.astype(BF16), b.astype(BF16), preferred_element_type=F32)


def _dot_f32(a, b):
    return jnp.dot(a, b, precision=HIGHEST, preferred_element_type=F32)


def _dot_nt(a, b, exact=False):
    dn = (((1,), (1,)), ((), ()))
    if exact:
        return lax.dot_general(a, b, dn, precision=HIGHEST, preferred_element_type=F32)
    return lax.dot_general(a.astype(BF16), b.astype(BF16), dn, preferred_element_type=F32)


def _dot_tn(a, b, exact=False):
    dn = (((0,), (0,)), ((), ()))
    if exact:
        return lax.dot_general(a, b, dn, precision=HIGHEST, preferred_element_type=F32)
    return lax.dot_general(a.astype(BF16), b.astype(BF16), dn, preferred_element_type=F32)


def _sigmoid(x):
    return jax.nn.sigmoid(x)


def _pick(n, prefs):
    for p in prefs:
        if n % p == 0:
            return p
    return n


def _adaln_kernel(cond_ref, down_ref, up_ref, bias_ref, o_ref):
    cnd = cond_ref[...]
    t = _dot_f32(cnd * _sigmoid(cnd), down_ref[...])
    o_ref[...] = _dot_f32(t, up_ref[...]) + bias_ref[...]


def _adaln(cond8, down, up, bias):
    depth, d, r = down.shape
    n = up.shape[2]
    tn = _pick(n, (2048, 1024, 512, 256, 128))
    return pl.pallas_call(
        _adaln_kernel,
        grid=(depth, n // tn),
        in_specs=[
            pl.BlockSpec((SUBLANES, d), lambda l, j: (0, 0)),
            pl.BlockSpec((None, d, r), lambda l, j: (l, 0, 0)),
            pl.BlockSpec((None, r, tn), lambda l, j: (l, 0, j)),
            pl.BlockSpec((None, 1, tn), lambda l, j: (l, 0, j)),
        ],
        out_specs=pl.BlockSpec((None, SUBLANES, tn), lambda l, j: (l, 0, j)),
        out_shape=jax.ShapeDtypeStruct((depth, SUBLANES, n), F32),
        compiler_params=_params("arbitrary", "arbitrary"),
        name="adaln",
    )(cond8, down, up, bias.reshape(depth, 1, n))


def _rms(x):
    return x * lax.rsqrt(jnp.mean(x * x, axis=-1, keepdims=True) + NORM_EPS)


def _norm_mod_kernel(x_ref, g_ref, mod_ref, o_ref, *, g_row, shift_row, scale_row):
    y = _rms(x_ref[...]) * g_ref[g_row:g_row + 1, :]
    y = y * (1.0 + mod_ref[scale_row:scale_row + 1, :]) + mod_ref[shift_row:shift_row + 1, :]
    o_ref[...] = y.astype(o_ref.dtype)


def _resid_kernel(x_ref, y_ref, g_ref, mod_ref, o_ref, *, g_row, gate_row):
    yn = _rms(y_ref[...]) * g_ref[g_row:g_row + 1, :]
    o_ref[...] = x_ref[...] + mod_ref[gate_row:gate_row + 1, :] * yn


def _row_specs(seq, tr, d, layer):
    n_lat = seq // tr
    tok = pl.BlockSpec((None, tr, d), lambda b, i: (b, i, 0))
    g = pl.BlockSpec((None, 4, d), lambda b, i: (layer, 0, 0))
    mod = pl.BlockSpec((None, None, None, N_MOD, d),
                       lambda b, i: (layer, b, jnp.where(i < n_lat, 0, 1), 0, 0))
    return tok, g, mod


def _norm_mod(xs, norm_g, mods, layer, seq, tr, g_row, shift_row, scale_row):
    b, t, d = xs.shape
    tok, g, mod = _row_specs(seq, tr, d, layer)
    return pl.pallas_call(
        functools.partial(_norm_mod_kernel, g_row=g_row, shift_row=shift_row, scale_row=scale_row),
        grid=(b, t // tr),
        in_specs=[tok, g, mod],
        out_specs=tok,
        out_shape=jax.ShapeDtypeStruct((b, t, d), BF16),
        compiler_params=_params("parallel", "parallel"),
        name="norm_mod",
    )(xs, norm_g, mods)


def _resid(xs, y, norm_g, mods, layer, seq, tr, g_row, gate_row):
    b, t, d = xs.shape
    tok, g, mod = _row_specs(seq, tr, d, layer)
    return pl.pallas_call(
        functools.partial(_resid_kernel, g_row=g_row, gate_row=gate_row),
        grid=(b, t // tr),
        in_specs=[tok, tok, g, mod],
        out_specs=tok,
        out_shape=jax.ShapeDtypeStruct((b, t, d), F32),
        compiler_params=_params("parallel", "parallel"),
        name="resid",
    )(xs, y.reshape(b, t, d), norm_g, mods)


def _mm_kernel(*refs, n_in):
    a_refs, w_refs, o_ref = refs[:n_in], refs[n_in:2 * n_in], refs[2 * n_in]
    acc = _dot(a_refs[0][...], w_refs[0][...])
    for a_ref, w_ref in zip(a_refs[1:], w_refs[1:]):
        acc = acc + _dot(a_ref[...], w_ref[...])
    o_ref[...] = acc.astype(o_ref.dtype)


def _mm(a_list, w, col0, ncols, out_dtype, tm=512, tn=512):
    m = a_list[0].shape[0]
    tm = _pick(m, (tm, 256, 128))
    tn = next(t for t in (tn, 256, 128) if ncols % t == 0 and col0 % t == 0)
    n_in = len(a_list)
    kw = a_list[0].shape[1]
    a_specs = [pl.BlockSpec((tm, kw), lambda i, j: (i, 0)) for _ in a_list]
    w_specs = [pl.BlockSpec((kw, tn), functools.partial(lambda i, j, r: (r, col0 // tn + j), r=r))
               for r in range(n_in)]
    return pl.pallas_call(
        functools.partial(_mm_kernel, n_in=n_in),
        grid=(m // tm, ncols // tn),
        in_specs=a_specs + w_specs,
        out_specs=pl.BlockSpec((tm, tn), lambda i, j: (i, j)),
        out_shape=jax.ShapeDtypeStruct((m, ncols), out_dtype),
        compiler_params=_params("parallel", "parallel"),
        name="proj",
    )(*a_list, *([w] * n_in))


def _swiglu_kernel(a_ref, w1_ref, w3_ref, o_ref):
    a = a_ref[...]
    g = _dot(a, w1_ref[...])
    u = _dot(a, w3_ref[...])
    o_ref[...] = (g * _sigmoid(g) * u).astype(o_ref.dtype)


def _swiglu_up(a, w1, w3, tm=512, tn=256):
    m, k = a.shape
    n = w1.shape[1]
    tm = _pick(m, (tm, 256, 128))
    tn = _pick(n, (tn, 128))
    return pl.pallas_call(
        _swiglu_kernel,
        grid=(m // tm, n // tn),
        in_specs=[pl.BlockSpec((tm, k), lambda i, j: (i, 0)),
                  pl.BlockSpec((k, tn), lambda i, j: (0, j)),
                  pl.BlockSpec((k, tn), lambda i, j: (0, j))],
        out_specs=pl.BlockSpec((tm, tn), lambda i, j: (i, j)),
        out_shape=jax.ShapeDtypeStruct((m, n), BF16),
        compiler_params=_params("parallel", "parallel"),
        name="swiglu_up",
    )(a, w1, w3)


def _mm_acc_kernel(a_ref, w_ref, o_ref, acc_ref):
    k = pl.program_id(2)

    @pl.when(k == 0)
    def _():
        acc_ref[...] = jnp.zeros_like(acc_ref)

    acc_ref[...] += _dot(a_ref[...], w_ref[...])

    @pl.when(k == pl.num_programs(2) - 1)
    def _():
        o_ref[...] = acc_ref[...]


def _mm_ksplit(a, w, tm=512, tn=512):
    m, kdim = a.shape
    n = w.shape[1]
    tm = _pick(m, (tm, 256, 128))
    tn = _pick(n, (tn, 256, 128))
    tk = kdim // 2 if (kdim // 2) % LANES == 0 else kdim
    return pl.pallas_call(
        _mm_acc_kernel,
        grid=(m // tm, n // tn, kdim // tk),
        in_specs=[pl.BlockSpec((tm, tk), lambda i, j, k: (i, k)),
                  pl.BlockSpec((tk, tn), lambda i, j, k: (k, j))],
        out_specs=pl.BlockSpec((tm, tn), lambda i, j, k: (i, j)),
        out_shape=jax.ShapeDtypeStruct((m, n), F32),
        scratch_shapes=[pltpu.VMEM((tm, tn), F32)],
        compiler_params=_params("parallel", "parallel", "arbitrary"),
        name="ffn_down",
    )(a, w)


def _neighbours(x, seq):
    t = x.shape[0]
    row = lax.broadcasted_iota(jnp.int32, x.shape, 0)
    prev = jnp.where(row == 0, 0.0, jnp.where(row == seq, 0.0, pltpu.roll(x, 1, 0)))
    nxt = jnp.where(row == seq - 1, 0.0, jnp.where(row == t - 1, 0.0, pltpu.roll(x, t - 1, 0)))
    return prev, nxt


def _short_conv_kernel(z_ref, p_ref, o_ref, *, seq):
    z = z_ref[...]
    prev, nxt = _neighbours(z, seq)
    o_ref[...] = p_ref[0:1, :] * prev + p_ref[1:2, :] * z + p_ref[2:3, :] * nxt + p_ref[3:4, :]


def _token_shift_kernel(z_ref, p_ref, o_ref, *, seq):
    z = z_ref[...]
    prev, nxt = _neighbours(z, seq)
    o_ref[...] = z + p_ref[...] * (0.5 * (prev + nxt) - z)


def _time_mix(body, z, params, seq, name):
    b, t, cw = z.shape
    tc = LANES
    return pl.pallas_call(
        functools.partial(body, seq=seq),
        grid=(b, cw // tc),
        in_specs=[pl.BlockSpec((None, t, tc), lambda i, j: (i, 0, j)),
                  pl.BlockSpec((params.shape[0], tc), lambda i, j: (0, j))],
        out_specs=pl.BlockSpec((None, t, tc), lambda i, j: (i, 0, j)),
        out_shape=jax.ShapeDtypeStruct((b, t, cw), F32),
        compiler_params=_params("parallel", "parallel"),
        name=name,
    )(z, params)


def _hy_hidden_kernel(bands_ref, w1t_ref, w1c_ref, w1s_ref, b1_ref, w2_ref, b2_ref, freq_ref, o_ref, *, length):
    hp = o_ref.shape[1]
    pos_b = lax.broadcasted_iota(jnp.int32, (length, bands_ref.shape[1]), 0).astype(F32)
    ang = (2 * math.pi) * pos_b / length * bands_ref[...]
    t = lax.broadcasted_iota(jnp.int32, (length, hp), 0).astype(F32) / max(length - 1, 1)
    pre = (t * w1t_ref[...] + _dot_f32(jnp.cos(ang), w1c_ref[...]) + _dot_f32(-jnp.sin(ang), w1s_ref[...])
           + b1_ref[...])
    h = jnp.sin(freq_ref[0:1, :] * pre)
    o_ref[...] = jnp.sin(freq_ref[1:2, :] * (_dot_f32(h, w2_ref[...]) + b2_ref[...]))


def _hy_hidden(length, bands, w1t, w1c, w1s, b1, w2, b2, freq):
    hp = w2.shape[0]
    full = lambda a: pl.BlockSpec(a.shape, lambda i: (0,) * a.ndim)
    args = (bands, w1t, w1c, w1s, b1, w2, b2, freq)
    return pl.pallas_call(
        functools.partial(_hy_hidden_kernel, length=length),
        grid=(1,),
        in_specs=[full(a) for a in args],
        out_specs=pl.BlockSpec((length, hp), lambda i: (0, 0)),
        out_shape=jax.ShapeDtypeStruct((length, hp), F32),
        compiler_params=_params("arbitrary"),
        name="hy_hidden",
    )(*args)


def _hy_taps_kernel(h_ref, w3_ref, delta_ref, o_ref, *, length):
    t = lax.broadcasted_iota(jnp.int32, o_ref.shape, 0).astype(F32) / max(length - 1, 1)
    o_ref[...] = _dot_f32(h_ref[...], w3_ref[...]) * jnp.exp(-t * delta_ref[...])


def _hy_taps(hmid, w3p, delta, c):
    length, hp = hmid.shape
    tc = _pick(c, (256, 128))
    nct = c // tc
    return pl.pallas_call(
        functools.partial(_hy_taps_kernel, length=length),
        grid=(4, nct),
        in_specs=[pl.BlockSpec((length, hp), lambda g, j: (0, 0)),
                  pl.BlockSpec((hp, tc), lambda g, j: (0, g * nct + j)),
                  pl.BlockSpec((1, tc), lambda g, j: (0, j))],
        out_specs=pl.BlockSpec((None, length, tc), lambda g, j: (g, 0, j)),
        out_shape=jax.ShapeDtypeStruct((4, length, c), F32),
        compiler_params=_params("parallel", "parallel"),
        name="hy_taps",
    )(hmid, w3p, delta)


def _fft_tables(n1, n2):
    n = n1 * n2
    n1h = n1 // 2

    def cs(phase, period):
        ang = (phase % period).astype(F32) * (2 * math.pi / period)
        return jnp.cos(ang), jnp.sin(ang)

    k1 = jnp.arange(n1, dtype=jnp.int32)
    c, s = cs(k1[:, None] * jnp.arange(n1h, dtype=jnp.int32)[None, :], n1)
    f_a = jnp.concatenate([c, -s], axis=0)
    c, s = cs(jnp.arange(n1h, dtype=jnp.int32)[:, None] * k1[None, :], n1)
    f_c = jnp.concatenate([c, -s], axis=1) / n
    k2 = jnp.arange(n2, dtype=jnp.int32)
    freq = k1[:, None, None] + n1 * k2[None, :, None]
    c, s = cs(freq * k2[None, None, :], n)
    g = jnp.concatenate([jnp.concatenate([c, s], axis=2),
                         jnp.concatenate([-s, c], axis=2)], axis=1)
    ct, st = jnp.swapaxes(c, 1, 2), jnp.swapaxes(s, 1, 2)
    gh = jnp.concatenate([jnp.concatenate([ct, -st], axis=2),
                          jnp.concatenate([st, ct], axis=2)], axis=1)
    return f_a.astype(BF16), f_c.astype(BF16), g.astype(BF16), gh.astype(BF16)


def _fft_a_kernel(u_ref, f_ref, ar_ref, ai_ref):
    n1 = ar_ref.shape[0]
    res = _dot(f_ref[...], u_ref[...])
    ar_ref[...] = res[:n1].astype(BF16)
    ai_ref[...] = res[n1:].astype(BF16)


def _merged(groups, g):
    return lambda rows, c: pl.BlockSpec((None, rows, c), lambda b, j: (b, 0, j * groups + g))


def _split(g):
    return lambda rows, c: pl.BlockSpec((None, None, rows, c), lambda b, j: (b, j, 0, g))


def _fft_a(src, layout, c, n2, f_a):
    bn = src.shape[0]
    n1 = f_a.shape[0] // 2
    out = jax.ShapeDtypeStruct((bn, n1, n2 * c), BF16)
    ospec = pl.BlockSpec((None, n1, c), lambda b, j: (b, 0, j))
    ar, ai = pl.pallas_call(
        _fft_a_kernel,
        grid=(bn, n2),
        in_specs=[layout(n1 // 2, c), pl.BlockSpec(f_a.shape, lambda b, j: (0, 0))],
        out_specs=[ospec, ospec],
        out_shape=[out, out],
        compiler_params=_params("parallel", "parallel"),
        name="fft_a",
    )(src, f_a)
    return ar.reshape(bn, n1, n2, c), ai.reshape(bn, n1, n2, c)


def _stack(r_ref, i_ref):
    return jnp.concatenate([r_ref[...], i_ref[...]], axis=0)


def _fft_spec_kernel(fr_ref, fi_ref, br_ref, bi_ref, g_ref, kr_ref, ki_ref):
    n2 = kr_ref.shape[0]
    xf = jnp.dot(g_ref[...], _stack(fr_ref, fi_ref), preferred_element_type=F32)
    xb = jnp.dot(g_ref[...], _stack(br_ref, bi_ref), preferred_element_type=F32)
    kr_ref[...] = xf[:n2] + xb[:n2]
    ki_ref[...] = xf[n2:] - xb[n2:]


def _fft_spec(tr, ti, g):
    _, n1, n2, c = tr.shape
    fwd = pl.BlockSpec((None, None, n2, c), lambda o, k: (o, k, 0, 0))
    bwd = pl.BlockSpec((None, None, n2, c), lambda o, k: (2 + o, k, 0, 0))
    out = jax.ShapeDtypeStruct((2, n1, n2, c), F32)
    return pl.pallas_call(
        _fft_spec_kernel,
        grid=(2, n1),
        in_specs=[fwd, fwd, bwd, bwd, pl.BlockSpec((None, 2 * n2, 2 * n2), lambda o, k: (k, 0, 0))],
        out_specs=[fwd, fwd],
        out_shape=[out, out],
        compiler_params=_params("parallel", "parallel"),
        name="fft_spec",
    )(tr, ti, tr, ti, g)


def _fft_mid_kernel(ar_ref, ai_ref, g_ref, gh_ref, kr_ref, ki_ref, zr_ref, zi_ref):
    n2 = zr_ref.shape[0]
    x = jnp.dot(g_ref[...], _stack(ar_ref, ai_ref), preferred_element_type=F32)
    xr, xi = x[:n2], x[n2:]
    kr, ki = kr_ref[...], ki_ref[...]
    y = jnp.concatenate([(xr * kr - xi * ki).astype(BF16), (xr * ki + xi * kr).astype(BF16)], axis=0)
    z = jnp.dot(gh_ref[...], y, preferred_element_type=F32)
    zr_ref[...] = z[:n2].astype(BF16)
    zi_ref[...] = z[n2:].astype(BF16)


def _fft_mid(ar, ai, g, gh, kr, ki, order):
    bn, n1, n2, c = ar.shape
    act = pl.BlockSpec((None, None, n2, c), lambda k, b: (b, k, 0, 0))
    mat = pl.BlockSpec((None, 2 * n2, 2 * n2), lambda k, b: (k, 0, 0))
    spec = pl.BlockSpec((None, None, n2, c), lambda k, b: (order, k, 0, 0))
    out = jax.ShapeDtypeStruct((bn, n1, n2, c), BF16)
    return pl.pallas_call(
        _fft_mid_kernel,
        grid=(n1, bn),
        in_specs=[act, act, mat, mat, spec, spec],
        out_specs=[act, act],
        out_shape=[out, out],
        compiler_params=_params("parallel", "parallel"),
        name="fft_mid",
    )(ar, ai, g, gh, kr, ki)


def _fft_c_kernel(zr_ref, zi_ref, f_ref, u_ref, gate_ref, bias_ref, o_ref):
    conv = jnp.dot(f_ref[...], _stack(zr_ref, zi_ref), preferred_element_type=F32)
    u = u_ref[...]
    o_ref[...] = gate_ref[...] * (conv + bias_ref[...] * u)


def _fft_c(zr, zi, f_c, u, u_layout, gate, gate_layout, bias, order):
    bn, n1, n2, c = zr.shape
    zspec = pl.BlockSpec((None, n1, c), lambda b, j: (b, 0, j))
    return pl.pallas_call(
        _fft_c_kernel,
        grid=(bn, n2),
        in_specs=[zspec, zspec, pl.BlockSpec(f_c.shape, lambda b, j: (0, 0)), u_layout(n1 // 2, c),
                  gate_layout(n1 // 2, c), pl.BlockSpec((None, 1, c), lambda b, j: (order, 0, 0))],
        out_specs=_merged(1, 0)(n1 // 2, c),
        out_shape=jax.ShapeDtypeStruct((bn, n1 // 2, n2 * c), F32),
        compiler_params=_params("parallel", "parallel"),
        name="fft_c",
    )(zr.reshape(bn, n1, n2 * c), zi.reshape(bn, n1, n2 * c), f_c, u, gate, bias)


def _hy_ctx_kernel(x1_ref, x2_ref, v_ref, taps_ref, bias_ref, fwd_ref, inv_ref, o_ref):
    n = fwd_ref.shape[0] // 2
    fwd, inv = fwd_ref[...], inv_ref[...]

    def spec(a):
        s = _dot_f32(fwd, a)
        return s[:n], s[n:]

    u = v_ref[...]
    for o, gate_ref in ((0, x1_ref), (1, x2_ref)):
        fr, fi = spec(taps_ref[o])
        br, bi = spec(taps_ref[2 + o])
        kr, ki = fr + br, fi - bi
        ur, ui = spec(u)
        y = jnp.concatenate([ur * kr - ui * ki, ur * ki + ui * kr], axis=0)
        u = gate_ref[...] * (_dot_f32(inv, y) + bias_ref[o:o + 1, :] * u)
    o_ref[...] = u


def _hy_ctx(zc, taps, bias, seq, ctx_len, c):
    b = zc.shape[0]
    n = 2 * ctx_len
    tc = _pick(c, (256, 128))
    nct = c // tc
    k = jnp.arange(n, dtype=jnp.int32)[:, None]
    m = jnp.arange(ctx_len, dtype=jnp.int32)[None, :]
    ang = ((k * m) % n).astype(F32) * (2 * math.pi / n)
    fwd = jnp.concatenate([jnp.cos(ang), -jnp.sin(ang)], axis=0)
    inv = jnp.concatenate([jnp.cos(ang).T, -jnp.sin(ang).T], axis=1) / n
    row = seq // ctx_len
    tok = lambda g: pl.BlockSpec((None, ctx_len, tc), lambda j, i: (i, row, g * nct + j))
    return pl.pallas_call(
        _hy_ctx_kernel,
        grid=(nct, b),
        in_specs=[tok(0), tok(1), tok(2),
                  pl.BlockSpec((4, ctx_len, tc), lambda j, i: (0, 0, j)),
                  pl.BlockSpec((2, tc), lambda j, i: (0, j)),
                  pl.BlockSpec(fwd.shape, lambda j, i: (0, 0)),
                  pl.BlockSpec(inv.shape, lambda j, i: (0, 0))],
        out_specs=pl.BlockSpec((None, ctx_len, tc), lambda j, i: (i, 0, j)),
        out_shape=jax.ShapeDtypeStruct((b, ctx_len, c), F32),
        compiler_params=_params("parallel", "parallel"),
        name="hy_ctx",
    )(zc, zc, zc, taps, bias, fwd, inv)


def _hyena(zh, hy, seq, ctx_len, tables):
    short_w, short_b, w1, b1, w2, b2, freq, w3, bias = hy
    b, t, c3 = zh.shape
    c = c3 // 3
    f_a, f_c, g, gh = tables
    n2 = LANES
    zc = _time_mix(_short_conv_kernel, zh, jnp.concatenate([short_w, short_b[None]], axis=0), seq, "short_conv")

    hf = w2.shape[0]
    hp = LANES
    pad_c = lambda a: jnp.pad(a, ((0, 0), (0, hp - hf)))
    bands = jnp.pad(jnp.linspace(1e-4, HY_BANDS - 1, HY_BANDS, dtype=F32)[None, :], ((0, 0), (0, LANES - HY_BANDS)))
    w1c = jnp.pad(w1[1:1 + HY_BANDS], ((0, LANES - HY_BANDS), (0, hp - hf)))
    w1s = jnp.pad(w1[1 + HY_BANDS:], ((0, LANES - HY_BANDS), (0, hp - hf)))
    hid_args = (bands, pad_c(w1[0:1]), w1c, w1s, pad_c(b1[None]), jnp.pad(w2, ((0, hp - hf), (0, hp - hf))),
                pad_c(b2[None]), pad_c(freq))
    w3p = jnp.pad(w3, ((0, hp - hf), (0, 0)))
    max_decay = math.log(HY_TARGET) / HY_FAST_PCT
    min_decay = math.log(HY_TARGET) / HY_SLOW_PCT
    delta = jnp.abs(jnp.linspace(min_decay, max_decay, c, dtype=F32))[None, :]

    taps_l = _hy_taps(_hy_hidden(seq, *hid_args), w3p, delta, c)
    taps_c = _hy_taps(_hy_hidden(ctx_len, *hid_args), w3p, delta, c)

    as_split = lambda a: jnp.swapaxes(a.reshape(a.shape[0], seq // n2, n2, a.shape[-1]), 1, 2)
    tr, ti = _fft_a(as_split(taps_l), _split(0), c, n2, f_a)
    kr, ki = _fft_spec(tr, ti, g)

    zct = as_split(zc[:, :seq])
    bias3 = bias.reshape(2, 1, c)
    ar, ai = _fft_a(zct, _split(2), c, n2, f_a)
    zr, zi = _fft_mid(ar, ai, g, gh, kr, ki, 0)
    u1 = _fft_c(zr, zi, f_c, zct, _split(2), zct, _split(0), bias3, 0)
    ar, ai = _fft_a(u1, _merged(1, 0), c, n2, f_a)
    zr, zi = _fft_mid(ar, ai, g, gh, kr, ki, 1)
    u2 = _fft_c(zr, zi, f_c, u1, _merged(1, 0), zct, _split(1), bias3, 1)
    out_c = _hy_ctx(zc, taps_c, bias, seq, ctx_len, c)
    return jnp.concatenate([u2.reshape(b, seq, c), out_c], axis=1)


def _head_sum(x, n_heads):
    fold = x[:, :LANES]
    for s in range(LANES, x.shape[1], LANES):
        fold = fold + x[:, s:s + LANES]
    r = lax.broadcasted_iota(jnp.int32, (LANES, LANES), 0) & (n_heads - 1)
    c = lax.broadcasted_iota(jnp.int32, (LANES, LANES), 1) & (n_heads - 1)
    per_head = _dot_f32(fold, jnp.where(r == c, 1.0, 0.0))
    return jnp.concatenate([per_head] * (x.shape[1] // LANES), axis=1)


def _rw_feat_kernel(r_ref, k_ref, v_ref, zl_ref, wup_ref, aup_ref, gup_ref, w0_ref, a0_ref, kkp_ref, kap_ref,
                    rk_ref, kk_o, g_o, bonus_o, w_o, kd_o, kka_o, *, n_heads):
    r, k, v = r_ref[...], k_ref[...], v_ref[...]
    zl = zl_ref[...]
    wl, al, gl = zl[:, :LANES], zl[:, LANES:2 * LANES], zl[:, 2 * LANES:]
    kq = k * kkp_ref[...]
    kk = kq * lax.rsqrt(_head_sum(kq * kq, n_heads) + 1e-12)
    kk_o[...] = kk
    g_o[...] = _dot(_sigmoid(gl), gup_ref[...])
    twl = jnp.tanh(wl)
    bonus = jnp.zeros_like(v)
    for d in range(2):
        w = jnp.exp(-RW_DECAY_SCALE * _sigmoid(w0_ref[d:d + 1, :] + _dot_f32(twl, wup_ref[d])))
        a = _sigmoid(a0_ref[d:d + 1, :] + _dot_f32(al, aup_ref[d]))
        kd = k * (1.0 + (a - 1.0) * kap_ref[...])
        w_o[d] = w
        kd_o[d] = kd
        kka_o[d] = kk * a
        bonus = bonus + _head_sum(r * kd * rk_ref[...], n_heads) * v
    bonus_o[...] = bonus


def _rw_features(zs, zl, w_up, a_up, g_up, w0, a0, k_k, k_a, r_k, tr):
    b, t, c3 = zs.shape
    c = c3 // 3
    tok = lambda g: pl.BlockSpec((None, tr, c), lambda i, j: (i, j, g))
    full = lambda a: pl.BlockSpec(a.shape, lambda i, j: (0,) * a.ndim)
    two = pl.BlockSpec((2, None, tr, c), lambda i, j: (0, i, j, 0))
    s1 = jax.ShapeDtypeStruct((b, t, c), F32)
    s2 = jax.ShapeDtypeStruct((2, b, t, c), F32)
    consts = (w_up, a_up, g_up, w0, a0, k_k, k_a, r_k)
    return pl.pallas_call(
        functools.partial(_rw_feat_kernel, n_heads=c // RW_HEAD),
        grid=(b, t // tr),
        in_specs=[tok(0), tok(1), tok(2), pl.BlockSpec((None, tr, zl.shape[2]), lambda i, j: (i, j, 0))]
        + [full(a) for a in consts],
        out_specs=[tok(0), tok(0), tok(0), two, two, two],
        out_shape=[s1, s1, s1, s2, s2, s2],
        compiler_params=_params("parallel", "parallel"),
        name="rw_features",
    )(zs, zs, zs, zl, *consts)


RW_STEPS = 32
RW_PARTIALS = 4


def _rw_scan_kernel(*refs, reverse, n_heads, accumulate):
    if accumulate:
        r_ref, w_ref, kd_ref, v_ref, kk_ref, kka_ref, yin_ref, y_ref, s_ref = refs
    else:
        r_ref, w_ref, kd_ref, v_ref, kk_ref, kka_ref, y_ref, s_ref = refs
    n_tiles = s_ref.shape[0]
    steps = v_ref.shape[0]

    @pl.when(pl.program_id(1) == 0)
    def _():
        s_ref[...] = jnp.zeros_like(s_ref)

    def total(parts):
        while len(parts) > 1:
            parts = [a + b for a, b in zip(parts[::2], parts[1::2])]
        x = parts[0].reshape(RW_HEAD, LANES)
        shift = LANES // 2
        while shift >= n_heads:
            x = x + pltpu.roll(x, shift, 1)
            shift //= 2
        return x.reshape(parts[0].shape)

    def accumulate_into(parts, j, term):
        parts[j % RW_PARTIALS] = term if parts[j % RW_PARTIALS] is None else parts[j % RW_PARTIALS] + term

    def group(gi, carry):
        g = (steps // SUBLANES - 1 - gi) if reverse else gi
        base = pl.multiple_of(g * SUBLANES, SUBLANES)
        for n in range(SUBLANES):
            jj = SUBLANES - 1 - n if reverse else n
            row = lambda ref, j: ref[pl.ds(base, SUBLANES), pl.ds(j * LANES, LANES)][jj:jj + 1][None]
            parts = [None] * RW_PARTIALS
            for j in range(n_tiles):
                accumulate_into(parts, j, s_ref[j] * row(kk_ref, j))
            removed = total(parts)
            v = v_ref[base + jj].reshape(removed.shape)
            parts = [None] * RW_PARTIALS
            for j in range(n_tiles):
                s = s_ref[j] * row(w_ref, j) - removed * row(kka_ref, j) + v * row(kd_ref, j)
                s_ref[j] = s
                accumulate_into(parts, j, s * row(r_ref, j))
            y = total(parts).reshape(v_ref.shape[1:])
            y_ref[base + jj] = (yin_ref[base + jj] + y) if accumulate else y
        return carry

    lax.fori_loop(0, steps // SUBLANES, group, 0)


def _segment_order(n_lat, n_ctx, reverse):
    if reverse:
        return lambda j: n_lat + n_ctx - 1 - j
    return lambda j: jnp.where(j < n_ctx, n_lat + j, j - n_ctx)


def _rw_scan(zs, w, kd, vt, kk, kka, seq, direction, y_prev):
    b, t, c = kk.shape
    reverse = direction == 1
    order = _segment_order(seq // RW_STEPS, (t - seq) // RW_STEPS, reverse)
    tok = pl.BlockSpec((None, RW_STEPS, c), lambda i, j: (i, order(j), 0))
    dtok = pl.BlockSpec((None, None, RW_STEPS, c), lambda i, j: (direction, i, order(j), 0))
    val = pl.BlockSpec((None, RW_STEPS, RW_HEAD, LANES), lambda i, j: (i, order(j), 0, 0))
    accumulate = y_prev is not None
    args = (zs, w, kd, vt, kk, kka) + ((y_prev,) if accumulate else ())
    return pl.pallas_call(
        functools.partial(_rw_scan_kernel, reverse=reverse, n_heads=c // RW_HEAD, accumulate=accumulate),
        grid=(b, t // RW_STEPS),
        in_specs=[tok, dtok, dtok, val, tok, dtok] + ([val] if accumulate else []),
        out_specs=val,
        out_shape=jax.ShapeDtypeStruct(vt.shape, F32),
        scratch_shapes=[pltpu.VMEM((c // LANES, RW_HEAD // SUBLANES, SUBLANES, LANES), F32)],
        input_output_aliases={6: 0} if accumulate else {},
        compiler_params=_params("parallel", "arbitrary"),
        name="rw_scan",
    )(*args)


def _rw_out_kernel(y_ref, bonus_ref, g_ref, lnw_ref, lnb_ref, o_ref, *, n_heads):
    y = y_ref[...]
    mean = _head_sum(y, n_heads) * (1.0 / RW_HEAD)
    yc = y - mean
    var = _head_sum(yc * yc, n_heads) * (1.0 / RW_HEAD)
    yn = yc * lax.rsqrt(var + RW_GN_EPS) * lnw_ref[...] + lnb_ref[...]
    o_ref[...] = ((yn + bonus_ref[...]) * g_ref[...]).astype(o_ref.dtype)


def _rw_out(y, bonus, g, ln_w, ln_b, tr):
    b, t, c = y.shape
    tok = pl.BlockSpec((None, tr, c), lambda i, j: (i, j, 0))
    vec = pl.BlockSpec((1, c), lambda i, j: (0, 0))
    return pl.pallas_call(
        functools.partial(_rw_out_kernel, n_heads=c // RW_HEAD),
        grid=(b, t // tr),
        in_specs=[tok, tok, tok, vec, vec],
        out_specs=tok,
        out_shape=jax.ShapeDtypeStruct((b, t, c), BF16),
        compiler_params=_params("parallel", "parallel"),
        name="rw_out",
    )(y, bonus, g, ln_w, ln_b)


def _pad_rows(a, rows):
    return jnp.pad(a, [(0, 0)] * (a.ndim - 2) + [(0, rows - a.shape[-2]), (0, 0)])


def _rw_perm(c):
    return jnp.arange(c).reshape(c // RW_HEAD, RW_HEAD).T.reshape(-1)


def _rwkv(zs_raw, zl_raw, rw, seq, tr):
    mu, w0, w_up, a0, a_up, g_up, k_k, k_a, r_k, ln_w, ln_b = rw
    b, t, _ = zs_raw.shape
    c = k_k.shape[0]
    n_heads = c // RW_HEAD
    perm = _rw_perm(c)
    pc = lambda a: jnp.take(a, perm, axis=-1)
    lw, la = w_up.shape[1], a_up.shape[1]
    mu_l = mu[3 * c:]
    pad_to = lambda a: jnp.pad(a, (0, LANES - a.shape[0]))
    mu_lora = jnp.concatenate([pad_to(mu_l[:lw]), pad_to(mu_l[lw:lw + la]), mu_l[lw + la:]])
    mu_main = pc(mu[:3 * c].reshape(3, c)).reshape(1, 3 * c)
    zs = _time_mix(_token_shift_kernel, zs_raw, mu_main, seq, "token_shift")
    zl = _time_mix(_token_shift_kernel, zl_raw, mu_lora[None, :], seq, "token_shift")
    kk, g, bonus, w, kd, kka = _rw_features(
        zs, zl, _pad_rows(pc(w_up), LANES), _pad_rows(pc(a_up), LANES), pc(g_up).astype(BF16), pc(w0), pc(a0),
        pc(k_k)[None], pc(k_a)[None], pc(r_k.reshape(c))[None], min(tr, 128))

    vt = jnp.tile(zs[..., 2 * c:].reshape(b, t, RW_HEAD, n_heads), (1, 1, 1, LANES // n_heads))
    y = _rw_scan(zs, w, kd, vt, kk, kka, seq, 0, None)
    y = _rw_scan(zs, w, kd, vt, kk, kka, seq, 1, y)
    y = y[..., :n_heads].reshape(b, t, c)
    return _rw_out(y, bonus, g, pc(ln_w)[None], pc(ln_b)[None], tr)


def _hg_bounds_kernel(x_ref, o_ref):
    x = x_ref[...]
    e = jnp.exp(x - jnp.max(x, axis=0, keepdims=True))
    p = e / jnp.sum(e, axis=0, keepdims=True)
    run = jnp.zeros_like(p[0:1])
    o_ref[0:1, :] = run
    for i in range(1, x.shape[0]):
        run = run + p[i:i + 1]
        o_ref[i:i + 1, :] = run


def _hg_bounds(lower):
    return pl.pallas_call(
        _hg_bounds_kernel,
        out_shape=jax.ShapeDtypeStruct(lower.shape, F32),
        name="hg_bounds",
    )(lower)


HG_HEADS_PER_STEP = 16


def _hg_scan_kernel(q_ref, f_ref, i_ref, lb_ref, o_ref, s_ref, *, reverse):
    cs = q_ref.shape[0]
    n_heads = s_ref.shape[0]

    @pl.when(pl.program_id(2) == 0)
    def _():
        s_ref[...] = jnp.zeros_like(s_ref)

    width = q_ref.shape[1]
    row = lax.broadcasted_iota(jnp.int32, (cs, cs), 0)
    col = lax.broadcasted_iota(jnp.int32, (cs, cs), 1)
    row_w = lax.broadcasted_iota(jnp.int32, (cs, width), 0)
    inclusive = lambda upto: jnp.where((col >= upto) if reverse else (col <= upto), 1.0, 0.0)
    sums = [inclusive(row)]
    levels = []
    hs = cs // 2
    while hs >= 1:
        blk = 2 * hs
        boundary = (row & ~(blk - 1)) + hs
        sums.append(inclusive(boundary if reverse else boundary - 1))
        levels.append(((row & ~(blk - 1)) == (col & ~(blk - 1)), (row_w & (blk - 1)) >= hs))
        hs //= 2

    heads = [slice(h * HG_HEAD, (h + 1) * HG_HEAD) for h in range(n_heads)]
    q, v, lb = q_ref[...], i_ref[...], lb_ref[...]
    f = lb + (1.0 - lb) * _sigmoid(f_ref[...])
    k = 1.0 - f
    bb = _dot_f32(jnp.concatenate(sums, axis=0), jnp.log(f))
    b = bb[:cs]
    qk = q * k
    att = [jnp.where(row == col, jnp.sum(qk[:, sl], axis=-1, keepdims=True), 0.0) for sl in heads]
    for l, (same, late) in enumerate(levels):
        ref_b = bb[(l + 1) * cs:(l + 2) * cs]
        q_side, k_side = (jnp.logical_not(late), late) if reverse else (late, jnp.logical_not(late))
        qt = (q * jnp.exp(jnp.where(q_side, b - ref_b, -jnp.inf))).astype(BF16)
        kt = (k * jnp.exp(jnp.where(k_side, ref_b - b, -jnp.inf))).astype(BF16)
        for h, sl in enumerate(heads):
            att[h] = att[h] + jnp.where(same, _dot_nt(qt[:, sl], kt[:, sl]), 0.0)
    b_end = b[0:1, :] if reverse else b[cs - 1:cs, :]
    qe = (q * jnp.exp(b)).astype(BF16)
    ke = (k * jnp.exp(b_end - b)).astype(BF16)
    decay = jnp.exp(b_end)
    vb = v.astype(BF16)
    for h, sl in enumerate(heads):
        st = s_ref[h]
        o_ref[:, sl] = _dot(att[h], vb[:, sl]) + _dot_nt(qe[:, sl], st)
        s_ref[h] = decay[:, sl] * st + _dot_tn(vb[:, sl], ke[:, sl])


def _hg_scan(p, lb, c, seq, direction):
    b, t, _ = p.shape
    wl = min(HG_HEADS_PER_STEP * HG_HEAD, c)
    ng = c // wl
    reverse = direction == 1
    order = _segment_order(seq // HG_CHUNK, (t - seq) // HG_CHUNK, reverse)
    blk = lambda g: pl.BlockSpec((None, HG_CHUNK, wl), lambda i, n, j: (i, order(j), g * ng + n))
    return pl.pallas_call(
        functools.partial(_hg_scan_kernel, reverse=reverse),
        grid=(b, ng, t // HG_CHUNK),
        in_specs=[blk(0), blk(1 + direction), blk(3), pl.BlockSpec((1, wl), lambda i, n, j: (0, n))],
        out_specs=blk(0),
        out_shape=jax.ShapeDtypeStruct((b, t, c), F32),
        scratch_shapes=[pltpu.VMEM((wl // HG_HEAD, HG_HEAD, HG_HEAD), F32)],
        compiler_params=_params("parallel", "parallel", "arbitrary"),
        name="hg_scan",
    )(p, p, p, lb)


def _hg_out_kernel(o0_ref, o1_ref, g_ref, w_ref, out_ref):
    o = o0_ref[...] + o1_ref[...]
    g = g_ref[...]
    for s in range(0, o.shape[1], HG_HEAD):
        oh = o[:, s:s + HG_HEAD]
        gh = g[:, s:s + HG_HEAD]
        out_ref[:, s:s + HG_HEAD] = (_rms(oh) * w_ref[...] * (gh * _sigmoid(gh))).astype(out_ref.dtype)


def _hg_out(o0, o1, p, norm_w, tr):
    b, t, c = o0.shape
    tc = _pick(c, (512, 256, 128))
    nct = c // tc
    tok = pl.BlockSpec((None, tr, tc), lambda i, j, n: (i, j, n))
    return pl.pallas_call(
        _hg_out_kernel,
        grid=(b, t // tr, nct),
        in_specs=[tok, tok, pl.BlockSpec((None, tr, tc), lambda i, j, n: (i, j, 4 * nct + n)),
                  pl.BlockSpec((1, HG_HEAD), lambda i, j, n: (0, 0))],
        out_specs=tok,
        out_shape=jax.ShapeDtypeStruct((b, t, c), BF16),
        compiler_params=_params("parallel", "parallel", "parallel"),
        name="hg_out",
    )(o0, o1, p, norm_w)


def _rope_tables(seq):
    t = jnp.arange(seq)
    axis = AT_HEAD // 2
    inv = ROPE_THETA ** (-jnp.arange(0, axis, 2, dtype=F32) / axis)
    ang = jnp.concatenate([(t // GRID_W)[:, None] * inv, (t % GRID_W)[:, None] * inv], axis=-1)
    sign = jnp.tile(jnp.array([-1.0, 1.0], F32), AT_HEAD // 2)
    return jnp.repeat(jnp.cos(ang), 2, axis=-1), jnp.repeat(jnp.sin(ang), 2, axis=-1) * sign


def _at_prep_kernel(q_ref, k_ref, qw_ref, kw_ref, cos_ref, sin_ref, qo_ref, ko_ref, *, n_lat):
    is_lat = pl.program_id(1) < n_lat
    cos, sin = cos_ref[...], sin_ref[...]
    even = (lax.broadcasted_iota(jnp.int32, cos.shape, 1) & 1) == 0

    def head(x, w):
        y = _rms(x) * w
        partner = jnp.where(even, pltpu.roll(y, AT_HEAD - 1, 1), pltpu.roll(y, 1, 1))
        return jnp.where(is_lat, y * cos + partner * sin, y)

    scale = AT_HEAD ** -0.5
    for s in range(0, qo_ref.shape[1], AT_HEAD):
        qo_ref[:, s:s + AT_HEAD] = (head(q_ref[:, s:s + AT_HEAD], qw_ref[...]) * scale).astype(qo_ref.dtype)
    for s in range(0, ko_ref.shape[1], AT_HEAD):
        ko_ref[:, s:s + AT_HEAD] = head(k_ref[:, s:s + AT_HEAD], kw_ref[...]).astype(ko_ref.dtype)


def _at_prep(p, c, q_norm, k_norm, cos, sin, seq, tr):
    b, t, _ = p.shape
    kvw = AT_KV_HEADS * AT_HEAD
    n_lat = seq // tr
    tab = pl.BlockSpec((tr, AT_HEAD), lambda i, j: (jnp.minimum(j, n_lat - 1), 0))
    vec = pl.BlockSpec((1, AT_HEAD), lambda i, j: (0, 0))
    return pl.pallas_call(
        functools.partial(_at_prep_kernel, n_lat=n_lat),
        grid=(b, t // tr),
        in_specs=[pl.BlockSpec((None, tr, c), lambda i, j: (i, j, 5)),
                  pl.BlockSpec((None, tr, kvw), lambda i, j: (i, j, 6 * c // kvw)),
                  vec, vec, tab, tab],
        out_specs=[pl.BlockSpec((None, tr, c), lambda i, j: (i, j, 0)),
                   pl.BlockSpec((None, tr, kvw), lambda i, j: (i, j, 0))],
        out_shape=[jax.ShapeDtypeStruct((b, t, c), BF16), jax.ShapeDtypeStruct((b, t, kvw), BF16)],
        compiler_params=_params("parallel", "parallel"),
        name="at_prep",
    )(p, p, q_norm, k_norm, cos, sin)


def _flash_kernel(q_ref, k_ref, v_ref, o_ref, m_ref, l_ref, acc_ref):
    kv = pl.program_id(3)

    @pl.when(kv == 0)
    def _():
        m_ref[...] = jnp.full_like(m_ref, -jnp.inf)
        l_ref[...] = jnp.zeros_like(l_ref)
        acc_ref[...] = jnp.zeros_like(acc_ref)

    k = k_ref[...]
    v = v_ref[...].astype(BF16)
    group = m_ref.shape[0]
    for r in range(group):
        s = _dot_nt(q_ref[:, r * AT_HEAD:(r + 1) * AT_HEAD], k)
        m_prev = m_ref[r]
        m_cur = jnp.maximum(m_prev, jnp.max(s, axis=-1, keepdims=True))
        alpha = jnp.exp(m_prev - m_cur)
        p = jnp.exp(s - m_cur[:, 0:1])
        l_ref[r] = alpha * l_ref[r] + jnp.sum(p, axis=-1, keepdims=True)
        acc_ref[r] = alpha * acc_ref[r] + _dot(p, v)
        m_ref[r] = m_cur

    @pl.when(kv == pl.num_programs(3) - 1)
    def _():
        for r in range(group):
            o_ref[:, r * AT_HEAD:(r + 1) * AT_HEAD] = (acc_ref[r] / l_ref[r]).astype(o_ref.dtype)


def _flash(qn, kn, p, c, q_blk0, tq, nq, kv_blk0, tk, nk):
    b = qn.shape[0]
    group = c // AT_HEAD // AT_KV_HEADS
    gw = group * AT_HEAD
    v_col0 = (6 * c + AT_KV_HEADS * AT_HEAD) // AT_HEAD
    return pl.pallas_call(
        _flash_kernel,
        grid=(b, AT_KV_HEADS, nq, nk),
        in_specs=[pl.BlockSpec((None, tq, gw), lambda i, g, a, n: (i, q_blk0 + a, g)),
                  pl.BlockSpec((None, tk, AT_HEAD), lambda i, g, a, n: (i, kv_blk0 + n, g)),
                  pl.BlockSpec((None, tk, AT_HEAD), lambda i, g, a, n: (i, kv_blk0 + n, v_col0 + g))],
        out_specs=pl.BlockSpec((None, tq, gw), lambda i, g, a, n: (i, a, g)),
        out_shape=jax.ShapeDtypeStruct((b, nq * tq, c), BF16),
        scratch_shapes=[pltpu.VMEM((group, tq, AT_HEAD), F32)] * 3,
        compiler_params=_params("parallel", "parallel", "parallel", "arbitrary"),
        name="flash",
    )(qn, kn, p)


def kernel(x, c, ctx, c_ctx, ada_down, ada_up, ada_bias, norm_g, ffn_w1, ffn_w3, ffn_w2, ev_w_in, ev_w_out, hy_short_w, hy_short_b, hy_pe_w1, hy_pe_b1, hy_pe_w2, hy_pe_b2, hy_sin_freq, hy_pe_w3, hy_bias, rw_mu, rw_w0, rw_w_up, rw_a0, rw_a_up, rw_g_up, rw_k_k, rw_k_a, rw_r_k, rw_ln_w, rw_ln_b, od_w_in, od_w_out, hg_lower_bounds, hg_norm_g, at_q_norm, at_k_norm):
    bsz, seq, d = x.shape
    ctx_len = ctx.shape[1]
    t = seq + ctx_len
    depth = ada_down.shape[0]
    half = d // 2
    tr = min(256, ctx_len)
    assert seq % ctx_len == 0 and ctx_len % LANES == 0 and bsz < SUBLANES

    xs = jnp.concatenate([x, ctx], axis=1)
    cond = jnp.concatenate([c, c_ctx[None], jnp.zeros((SUBLANES - bsz - 1, d), F32)], axis=0)
    m = _adaln(cond, ada_down, ada_up, ada_bias).reshape(depth, SUBLANES, N_MOD, d)
    mods = jnp.stack([m[:, :bsz], jnp.broadcast_to(m[:, bsz:bsz + 1], (depth, bsz, N_MOD, d))], axis=2)

    n1 = 2 * seq // LANES
    tables = _fft_tables(n1, LANES)
    cos, sin = _rope_tables(seq)
    lb_all = _hg_bounds(hg_lower_bounds)

    for l in range(depth):
        h = _norm_mod(xs, norm_g, mods, l, seq, tr, 0, 0, 1).reshape(bsz * t, d)
        if l % 2 == 0:
            e = l // 2
            w_in = ev_w_in[e]
            n_hy = 3 * half
            lw, la = rw_w_up.shape[2], rw_a_up.shape[2]
            w_lo = w_in[:, 2 * n_hy:]
            pad_c = lambda a: jnp.pad(a, ((0, 0), (0, LANES - a.shape[1])))
            w_lora = jnp.concatenate([pad_c(w_lo[:, :lw]), pad_c(w_lo[:, lw:lw + la]), w_lo[:, lw + la:]], axis=1)
            n_rw_heads = half // RW_HEAD
            w_rkv = w_in[:, n_hy:2 * n_hy].reshape(d, 3, n_rw_heads, RW_HEAD).swapaxes(2, 3).reshape(d, n_hy)
            w_main = jnp.concatenate([w_in[:, :n_hy], w_rkv], axis=1).astype(BF16)
            zh = _mm([h], w_main, 0, n_hy, F32).reshape(bsz, t, n_hy)
            zs = _mm([h], w_main, n_hy, n_hy, F32).reshape(bsz, t, n_hy)
            zl = _mm([h], w_lora.astype(BF16), 0, w_lora.shape[1], F32).reshape(bsz, t, -1)
            hy = (hy_short_w[e], hy_short_b[e], hy_pe_w1[e], hy_pe_b1[e], hy_pe_w2[e], hy_pe_b2[e],
                  hy_sin_freq[e], hy_pe_w3[e], hy_bias[e])
            rw = (rw_mu[e], rw_w0[e], rw_w_up[e], rw_a0[e], rw_a_up[e], rw_g_up[e], rw_k_k[e],
                  rw_k_a[e], rw_r_k[e], rw_ln_w[e], rw_ln_b[e])
            mix_a = _hyena(zh, hy, seq, ctx_len, tables).astype(BF16)
            mix_b = _rwkv(zs, zl, rw, seq, tr)
            w_rw_rows = ev_w_out[e][half:].reshape(n_rw_heads, RW_HEAD, d).swapaxes(0, 1).reshape(half, d)
            w_out = jnp.concatenate([ev_w_out[e][:half], w_rw_rows], axis=0)
        else:
            o = l // 2
            p = _mm([h], od_w_in[o].astype(BF16), 0, od_w_in.shape[2], F32).reshape(bsz, t, -1)
            lb = lb_all[l][None, :]
            o0 = _hg_scan(p, lb, half, seq, 0)
            o1 = _hg_scan(p, lb, half, seq, 1)
            mix_a = _hg_out(o0, o1, p, hg_norm_g[o][None], tr)
            qn, kn = _at_prep(p, half, at_q_norm[o][None], at_k_norm[o][None], cos, sin, seq, tr)
            tq = _pick(seq, (512, 256, 128))
            tk = _pick(t, (768, 512, 256, 128))
            at_l = _flash(qn, kn, p, half, 0, tq, seq // tq, 0, tk, t // tk)
            at_c = _flash(qn, kn, p, half, seq // ctx_len, ctx_len, 1, seq // ctx_len, ctx_len, 1)
            mix_b = jnp.concatenate([at_l, at_c], axis=1)
            w_out = od_w_out[o]
        y = _mm([mix_a.reshape(bsz * t, half), mix_b.reshape(bsz * t, half)], w_out.astype(BF16), 0, d, F32)
        xs = _resid(xs, y, norm_g, mods, l, seq, tr, 1, 2)
        h = _norm_mod(xs, norm_g, mods, l, seq, tr, 2, 3, 4).reshape(bsz * t, d)
        hid = _swiglu_up(h, ffn_w1[l].astype(BF16), ffn_w3[l].astype(BF16))
        y = _mm_ksplit(hid, ffn_w2[l].astype(BF16))
        xs = _resid(xs, y, norm_g, mods, l, seq, tr, 3, 5)
    return xs[:, :seq]
```

```python
import functools
import math

import numpy as np
import jax
import jax.numpy as jnp
from jax import lax
from jax.experimental import pallas as pl
from jax.experimental.pallas import tpu as pltpu

F32 = jnp.float32
BF16 = jnp.bfloat16
HIGHEST = lax.Precision.HIGHEST

NORM_EPS = 1e-6
N_MOD = 6
GRID_W = 64
HY_BANDS = 16
HY_TARGET = 1e-2
HY_FAST_PCT = 0.3
HY_SLOW_PCT = 1.5
RW_HEAD = 64
RW_DECAY_SCALE = math.exp(-0.5)
RW_GN_EPS = 64e-5
HG_HEAD = 128
HG_CHUNK = 64
AT_HEAD = 128
AT_KV_HEADS = 4
ROPE_THETA = 10000.0

LANES = 128
SUBLANES = 8
VMEM_LIMIT = 56 << 20


def _params(*sem):
    return pltpu.CompilerParams(dimension_semantics=sem, vmem_limit_bytes=VMEM_LIMIT)


def _dot(a, b):
    return jnp.dot(a.astype(BF16), b.astype(BF16), preferred_element_type=F32)


def _dot_f32(a, b):
    return jnp.dot(a, b, precision=HIGHEST, preferred_element_type=F32)


def _dot_nt(a, b, exact=False):
    dn = (((1,), (1,)), ((), ()))
    if exact:
        return lax.dot_general(a, b, dn, precision=HIGHEST, preferred_element_type=F32)
    return lax.dot_general(a.astype(BF16), b.astype(BF16), dn, preferred_element_type=F32)


def _dot_tn(a, b, exact=False):
    dn = (((0,), (0,)), ((), ()))
    if exact:
        return lax.dot_general(a, b, dn, precision=HIGHEST, preferred_element_type=F32)
    return lax.dot_general(a.astype(BF16), b.astype(BF16), dn, preferred_element_type=F32)


def _sigmoid(x):
    return jax.nn.sigmoid(x)


def _pick(n, prefs):
    for p in prefs:
        if n % p == 0:
            return p
    return n


def _adaln_kernel(cond_ref, down_ref, up_ref, bias_ref, o_ref):
    cnd = cond_ref[...]
    t = _dot_f32(cnd * _sigmoid(cnd), down_ref[...])
    o_ref[...] = _dot_f32(t, up_ref[...]) + bias_ref[...]


def _adaln(cond8, down, up, bias):
    depth, d, r = down.shape
    n = up.shape[2]
    tn = _pick(n, (2048, 1024, 512, 256, 128))
    return pl.pallas_call(
        _adaln_kernel,
        grid=(depth, n // tn),
        in_specs=[
            pl.BlockSpec((SUBLANES, d), lambda l, j: (0, 0)),
            pl.BlockSpec((None, d, r), lambda l, j: (l, 0, 0)),
            pl.BlockSpec((None, r, tn), lambda l, j: (l, 0, j)),
            pl.BlockSpec((None, 1, tn), lambda l, j: (l, 0, j)),
        ],
        out_specs=pl.BlockSpec((None, SUBLANES, tn), lambda l, j: (l, 0, j)),
        out_shape=jax.ShapeDtypeStruct((depth, SUBLANES, n), F32),
        compiler_params=_params("arbitrary", "arbitrary"),
        name="adaln",
    )(cond8, down, up, bias.reshape(depth, 1, n))


def _rms(x):
    return x * lax.rsqrt(jnp.mean(x * x, axis=-1, keepdims=True) + NORM_EPS)


def _norm_mod_kernel(x_ref, g_ref, mod_ref, o_ref, *, g_row, shift_row, scale_row):
    y = _rms(x_ref[...]) * g_ref[g_row:g_row + 1, :]
    y = y * (1.0 + mod_ref[scale_row:scale_row + 1, :]) + mod_ref[shift_row:shift_row + 1, :]
    o_ref[...] = y.astype(o_ref.dtype)


def _resid_kernel(x_ref, y_ref, g_ref, mod_ref, o_ref, *, g_row, gate_row):
    yn = _rms(y_ref[...]) * g_ref[g_row:g_row + 1, :]
    o_ref[...] = x_ref[...] + mod_ref[gate_row:gate_row + 1, :] * yn


def _row_specs(seq, tr, d, layer):
    n_lat = seq // tr
    tok = pl.BlockSpec((None, tr, d), lambda b, i: (b, i, 0))
    g = pl.BlockSpec((None, 4, d), lambda b, i: (layer, 0, 0))
    mod = pl.BlockSpec((None, None, None, N_MOD, d),
                       lambda b, i: (layer, b, jnp.where(i < n_lat, 0, 1), 0, 0))
    return tok, g, mod


def _norm_mod(xs, norm_g, mods, layer, seq, tr, g_row, shift_row, scale_row):
    b, t, d = xs.shape
    tok, g, mod = _row_specs(seq, tr, d, layer)
    return pl.pallas_call(
        functools.partial(_norm_mod_kernel, g_row=g_row, shift_row=shift_row, scale_row=scale_row),
        grid=(b, t // tr),
        in_specs=[tok, g, mod],
        out_specs=tok,
        out_shape=jax.ShapeDtypeStruct((b, t, d), BF16),
        compiler_params=_params("parallel", "parallel"),
        name="norm_mod",
    )(xs, norm_g, mods)


def _resid(xs, y, norm_g, mods, layer, seq, tr, g_row, gate_row):
    b, t, d = xs.shape
    tok, g, mod = _row_specs(seq, tr, d, layer)
    return pl.pallas_call(
        functools.partial(_resid_kernel, g_row=g_row, gate_row=gate_row),
        grid=(b, t // tr),
        in_specs=[tok, tok, g, mod],
        out_specs=tok,
        out_shape=jax.ShapeDtypeStruct((b, t, d), F32),
        compiler_params=_params("parallel", "parallel"),
        name="resid",
    )(xs, y.reshape(b, t, d), norm_g, mods)


def _mm_kernel(*refs, n_in):
    a_refs, w_refs, o_ref = refs[:n_in], refs[n_in:2 * n_in], refs[2 * n_in]
    acc = _dot(a_refs[0][...], w_refs[0][...])
    for a_ref, w_ref in zip(a_refs[1:], w_refs[1:]):
        acc = acc + _dot(a_ref[...], w_ref[...])
    o_ref[...] = acc.astype(o_ref.dtype)


def _mm(a_list, w, col0, ncols, out_dtype, tm=1536, tn=512):
    m = a_list[0].shape[0]
    tm = _pick(m, (tm, 512, 256, 128))
    tn = next(t for t in (tn, 256, 128) if ncols % t == 0 and col0 % t == 0)
    n_in = len(a_list)
    kw = a_list[0].shape[1]
    a_specs = [pl.BlockSpec((tm, kw), lambda i, j: (i, 0)) for _ in a_list]
    w_specs = [pl.BlockSpec((kw, tn), functools.partial(lambda i, j, r: (r, col0 // tn + j), r=r))
               for r in range(n_in)]
    return pl.pallas_call(
        functools.partial(_mm_kernel, n_in=n_in),
        grid=(m // tm, ncols // tn),
        in_specs=a_specs + w_specs,
        out_specs=pl.BlockSpec((tm, tn), lambda i, j: (i, j)),
        out_shape=jax.ShapeDtypeStruct((m, ncols), out_dtype),
        compiler_params=_params("parallel", "parallel"),
        name="proj",
    )(*a_list, *([w] * n_in))


def _swiglu_kernel(a_ref, w1_ref, w3_ref, o_ref):
    a = a_ref[...]
    g = _dot(a, w1_ref[...])
    u = _dot(a, w3_ref[...])
    o_ref[...] = (g * _sigmoid(g) * u).astype(o_ref.dtype)


def _swiglu_up(a, w1, w3, tm=1536, tn=256):
    m, k = a.shape
    n = w1.shape[1]
    tm = _pick(m, (tm, 512, 256, 128))
    tn = _pick(n, (tn, 128))
    return pl.pallas_call(
        _swiglu_kernel,
        grid=(m // tm, n // tn),
        in_specs=[pl.BlockSpec((tm, k), lambda i, j: (i, 0)),
                  pl.BlockSpec((k, tn), lambda i, j: (0, j)),
                  pl.BlockSpec((k, tn), lambda i, j: (0, j))],
        out_specs=pl.BlockSpec((tm, tn), lambda i, j: (i, j)),
        out_shape=jax.ShapeDtypeStruct((m, n), BF16),
        compiler_params=_params("parallel", "parallel"),
        name="swiglu_up",
    )(a, w1, w3)


def _mm_acc_kernel(a_ref, w_ref, o_ref, acc_ref):
    k = pl.program_id(2)

    @pl.when(k == 0)
    def _():
        acc_ref[...] = jnp.zeros_like(acc_ref)

    acc_ref[...] += _dot(a_ref[...], w_ref[...])

    @pl.when(k == pl.num_programs(2) - 1)
    def _():
        o_ref[...] = acc_ref[...]


def _mm_ksplit(a, w, tm=768, tn=1024):
    m, kdim = a.shape
    n = w.shape[1]
    tm = _pick(m, (tm, 512, 256, 128))
    tn = _pick(n, (tn, 512, 256, 128))
    tk = kdim // 2 if (kdim // 2) % LANES == 0 else kdim
    return pl.pallas_call(
        _mm_acc_kernel,
        grid=(m // tm, n // tn, kdim // tk),
        in_specs=[pl.BlockSpec((tm, tk), lambda i, j, k: (i, k)),
                  pl.BlockSpec((tk, tn), lambda i, j, k: (k, j))],
        out_specs=pl.BlockSpec((tm, tn), lambda i, j, k: (i, j)),
        out_shape=jax.ShapeDtypeStruct((m, n), F32),
        scratch_shapes=[pltpu.VMEM((tm, tn), F32)],
        compiler_params=_params("parallel", "parallel", "arbitrary"),
        name="ffn_down",
    )(a, w)


def _neighbours(x, seq):
    t = x.shape[0]
    row = lax.broadcasted_iota(jnp.int32, x.shape, 0)
    prev = jnp.where(row == 0, 0.0, jnp.where(row == seq, 0.0, pltpu.roll(x, 1, 0)))
    nxt = jnp.where(row == seq - 1, 0.0, jnp.where(row == t - 1, 0.0, pltpu.roll(x, t - 1, 0)))
    return prev, nxt


def _short_conv_kernel(z_ref, p_ref, o_ref, *, seq):
    z = z_ref[...]
    prev, nxt = _neighbours(z, seq)
    o_ref[...] = p_ref[0:1, :] * prev + p_ref[1:2, :] * z + p_ref[2:3, :] * nxt + p_ref[3:4, :]


def _token_shift_kernel(z_ref, p_ref, o_ref, *, seq):
    z = z_ref[...]
    prev, nxt = _neighbours(z, seq)
    o_ref[...] = z + p_ref[...] * (0.5 * (prev + nxt) - z)


def _time_mix(body, z, params, seq, name):
    b, t, cw = z.shape
    tc = LANES
    return pl.pallas_call(
        functools.partial(body, seq=seq),
        grid=(b, cw // tc),
        in_specs=[pl.BlockSpec((None, t, tc), lambda i, j: (i, 0, j)),
                  pl.BlockSpec((params.shape[0], tc), lambda i, j: (0, j))],
        out_specs=pl.BlockSpec((None, t, tc), lambda i, j: (i, 0, j)),
        out_shape=jax.ShapeDtypeStruct((b, t, cw), F32),
        compiler_params=_params("parallel", "parallel"),
        name=name,
    )(z, params)


def _hy_hidden_kernel(bands_ref, w1t_ref, w1c_ref, w1s_ref, b1_ref, w2_ref, b2_ref, freq_ref, o_ref, *, length):
    hp = o_ref.shape[1]
    pos_b = lax.broadcasted_iota(jnp.int32, (length, bands_ref.shape[1]), 0).astype(F32)
    ang = (2 * math.pi) * pos_b / length * bands_ref[...]
    t = lax.broadcasted_iota(jnp.int32, (length, hp), 0).astype(F32) / max(length - 1, 1)
    pre = (t * w1t_ref[...] + _dot_f32(jnp.cos(ang), w1c_ref[...]) + _dot_f32(-jnp.sin(ang), w1s_ref[...])
           + b1_ref[...])
    h = jnp.sin(freq_ref[0:1, :] * pre)
    o_ref[...] = jnp.sin(freq_ref[1:2, :] * (_dot_f32(h, w2_ref[...]) + b2_ref[...]))


def _hy_hidden(length, bands, w1t, w1c, w1s, b1, w2, b2, freq):
    hp = w2.shape[0]
    full = lambda a: pl.BlockSpec(a.shape, lambda i: (0,) * a.ndim)
    args = (bands, w1t, w1c, w1s, b1, w2, b2, freq)
    return pl.pallas_call(
        functools.partial(_hy_hidden_kernel, length=length),
        grid=(1,),
        in_specs=[full(a) for a in args],
        out_specs=pl.BlockSpec((length, hp), lambda i: (0, 0)),
        out_shape=jax.ShapeDtypeStruct((length, hp), F32),
        compiler_params=_params("arbitrary"),
        name="hy_hidden",
    )(*args)


def _hy_taps_kernel(h_ref, w3_ref, delta_ref, o_ref, *, length):
    t = lax.broadcasted_iota(jnp.int32, o_ref.shape, 0).astype(F32) / max(length - 1, 1)
    o_ref[...] = _dot_f32(h_ref[...], w3_ref[...]) * jnp.exp(-t * delta_ref[...])


def _hy_taps(hmid, w3p, delta, c):
    length, hp = hmid.shape
    tc = _pick(c, (256, 128))
    nct = c // tc
    return pl.pallas_call(
        functools.partial(_hy_taps_kernel, length=length),
        grid=(4, nct),
        in_specs=[pl.BlockSpec((length, hp), lambda g, j: (0, 0)),
                  pl.BlockSpec((hp, tc), lambda g, j: (0, g * nct + j)),
                  pl.BlockSpec((1, tc), lambda g, j: (0, j))],
        out_specs=pl.BlockSpec((None, length, tc), lambda g, j: (g, 0, j)),
        out_shape=jax.ShapeDtypeStruct((4, length, c), F32),
        compiler_params=_params("parallel", "parallel"),
        name="hy_taps",
    )(hmid, w3p, delta)


def _fft_tables(n1, n2):
    n = n1 * n2
    n1h = n1 // 2

    def cs(phase, period):
        ang = (phase % period).astype(F32) * (2 * math.pi / period)
        return jnp.cos(ang), jnp.sin(ang)

    k1 = jnp.arange(n1, dtype=jnp.int32)
    c, s = cs(k1[:, None] * jnp.arange(n1h, dtype=jnp.int32)[None, :], n1)
    f_a = jnp.concatenate([c, -s], axis=0)
    c, s = cs(jnp.arange(n1h, dtype=jnp.int32)[:, None] * k1[None, :], n1)
    f_c = jnp.concatenate([c, -s], axis=1) / n
    k2 = jnp.arange(n2, dtype=jnp.int32)
    freq = k1[:, None, None] + n1 * k2[None, :, None]
    c, s = cs(freq * k2[None, None, :], n)
    g = jnp.concatenate([jnp.concatenate([c, s], axis=2),
                         jnp.concatenate([-s, c], axis=2)], axis=1)
    ct, st = jnp.swapaxes(c, 1, 2), jnp.swapaxes(s, 1, 2)
    gh = jnp.concatenate([jnp.concatenate([ct, -st], axis=2),
                          jnp.concatenate([st, ct], axis=2)], axis=1)
    return f_a.astype(BF16), f_c.astype(BF16), g.astype(BF16), gh.astype(BF16)


def _fft_a_kernel(u_ref, f_ref, ar_ref, ai_ref):
    n1 = ar_ref.shape[0]
    res = _dot(f_ref[...], u_ref[...])
    ar_ref[...] = res[:n1].astype(BF16)
    ai_ref[...] = res[n1:].astype(BF16)


def _merged(groups, g):
    return lambda rows, c: pl.BlockSpec((None, rows, c), lambda b, j: (b, 0, j * groups + g))


def _split(g):
    return lambda rows, c: pl.BlockSpec((None, None, rows, c), lambda b, j: (b, j, 0, g))


def _fft_a(src, layout, c, n2, f_a):
    bn = src.shape[0]
    n1 = f_a.shape[0] // 2
    out = jax.ShapeDtypeStruct((bn, n1, n2 * c), BF16)
    ospec = pl.BlockSpec((None, n1, c), lambda b, j: (b, 0, j))
    ar, ai = pl.pallas_call(
        _fft_a_kernel,
        grid=(bn, n2),
        in_specs=[layout(n1 // 2, c), pl.BlockSpec(f_a.shape, lambda b, j: (0, 0))],
        out_specs=[ospec, ospec],
        out_shape=[out, out],
        compiler_params=_params("parallel", "parallel"),
        name="fft_a",
    )(src, f_a)
    return ar.reshape(bn, n1, n2, c), ai.reshape(bn, n1, n2, c)


def _stack(r_ref, i_ref):
    return jnp.concatenate([r_ref[...], i_ref[...]], axis=0)


def _fft_spec_kernel(fr_ref, fi_ref, br_ref, bi_ref, g_ref, kr_ref, ki_ref):
    n2 = kr_ref.shape[0]
    xf = jnp.dot(g_ref[...], _stack(fr_ref, fi_ref), preferred_element_type=F32)
    xb = jnp.dot(g_ref[...], _stack(br_ref, bi_ref), preferred_element_type=F32)
    kr_ref[...] = xf[:n2] + xb[:n2]
    ki_ref[...] = xf[n2:] - xb[n2:]


def _fft_spec(tr, ti, g):
    _, n1, n2, c = tr.shape
    fwd = pl.BlockSpec((None, None, n2, c), lambda o, k: (o, k, 0, 0))
    bwd = pl.BlockSpec((None, None, n2, c), lambda o, k: (2 + o, k, 0, 0))
    out = jax.ShapeDtypeStruct((2, n1, n2, c), F32)
    return pl.pallas_call(
        _fft_spec_kernel,
        grid=(2, n1),
        in_specs=[fwd, fwd, bwd, bwd, pl.BlockSpec((None, 2 * n2, 2 * n2), lambda o, k: (k, 0, 0))],
        out_specs=[fwd, fwd],
        out_shape=[out, out],
        compiler_params=_params("parallel", "parallel"),
        name="fft_spec",
    )(tr, ti, tr, ti, g)


def _fft_mid_kernel(ar_ref, ai_ref, g_ref, gh_ref, kr_ref, ki_ref, zr_ref, zi_ref):
    n2 = zr_ref.shape[0]
    x = jnp.dot(g_ref[...], _stack(ar_ref, ai_ref), preferred_element_type=F32)
    xr, xi = x[:n2], x[n2:]
    kr, ki = kr_ref[...], ki_ref[...]
    y = jnp.concatenate([(xr * kr - xi * ki).astype(BF16), (xr * ki + xi * kr).astype(BF16)], axis=0)
    z = jnp.dot(gh_ref[...], y, preferred_element_type=F32)
    zr_ref[...] = z[:n2].astype(BF16)
    zi_ref[...] = z[n2:].astype(BF16)


def _fft_mid(ar, ai, g, gh, kr, ki, order):
    bn, n1, n2, c = ar.shape
    act = pl.BlockSpec((None, None, n2, c), lambda k, b: (b, k, 0, 0))
    mat = pl.BlockSpec((None, 2 * n2, 2 * n2), lambda k, b: (k, 0, 0))
    spec = pl.BlockSpec((None, None, n2, c), lambda k, b: (order, k, 0, 0))
    out = jax.ShapeDtypeStruct((bn, n1, n2, c), BF16)
    return pl.pallas_call(
        _fft_mid_kernel,
        grid=(n1, bn),
        in_specs=[act, act, mat, mat, spec, spec],
        out_specs=[act, act],
        out_shape=[out, out],
        compiler_params=_params("parallel", "parallel"),
        name="fft_mid",
    )(ar, ai, g, gh, kr, ki)


def _fft_c_kernel(zr_ref, zi_ref, f_ref, u_ref, gate_ref, bias_ref, o_ref):
    conv = jnp.dot(f_ref[...], _stack(zr_ref, zi_ref), preferred_element_type=F32)
    u = u_ref[...]
    o_ref[...] = gate_ref[...] * (conv + bias_ref[...] * u)


def _fft_c(zr, zi, f_c, u, u_layout, gate, gate_layout, bias, order):
    bn, n1, n2, c = zr.shape
    zspec = pl.BlockSpec((None, n1, c), lambda b, j: (b, 0, j))
    return pl.pallas_call(
        _fft_c_kernel,
        grid=(bn, n2),
        in_specs=[zspec, zspec, pl.BlockSpec(f_c.shape, lambda b, j: (0, 0)), u_layout(n1 // 2, c),
                  gate_layout(n1 // 2, c), pl.BlockSpec((None, 1, c), lambda b, j: (order, 0, 0))],
        out_specs=_merged(1, 0)(n1 // 2, c),
        out_shape=jax.ShapeDtypeStruct((bn, n1 // 2, n2 * c), F32),
        compiler_params=_params("parallel", "parallel"),
        name="fft_c",
    )(zr.reshape(bn, n1, n2 * c), zi.reshape(bn, n1, n2 * c), f_c, u, gate, bias)


def _hy_ctx_kernel(x1_ref, x2_ref, v_ref, taps_ref, bias_ref, fwd_ref, inv_ref, o_ref):
    n = fwd_ref.shape[0] // 2
    fwd, inv = fwd_ref[...], inv_ref[...]

    def spec(a):
        s = _dot_f32(fwd, a)
        return s[:n], s[n:]

    u = v_ref[...]
    for o, gate_ref in ((0, x1_ref), (1, x2_ref)):
        fr, fi = spec(taps_ref[o])
        br, bi = spec(taps_ref[2 + o])
        kr, ki = fr + br, fi - bi
        ur, ui = spec(u)
        y = jnp.concatenate([ur * kr - ui * ki, ur * ki + ui * kr], axis=0)
        u = gate_ref[...] * (_dot_f32(inv, y) + bias_ref[o:o + 1, :] * u)
    o_ref[...] = u


def _hy_ctx(zc, taps, bias, seq, ctx_len, c):
    b = zc.shape[0]
    n = 2 * ctx_len
    tc = _pick(c, (256, 128))
    nct = c // tc
    k = jnp.arange(n, dtype=jnp.int32)[:, None]
    m = jnp.arange(ctx_len, dtype=jnp.int32)[None, :]
    ang = ((k * m) % n).astype(F32) * (2 * math.pi / n)
    fwd = jnp.concatenate([jnp.cos(ang), -jnp.sin(ang)], axis=0)
    inv = jnp.concatenate([jnp.cos(ang).T, -jnp.sin(ang).T], axis=1) / n
    row = seq // ctx_len
    tok = lambda g: pl.BlockSpec((None, ctx_len, tc), lambda j, i: (i, row, g * nct + j))
    return pl.pallas_call(
        _hy_ctx_kernel,
        grid=(nct, b),
        in_specs=[tok(0), tok(1), tok(2),
                  pl.BlockSpec((4, ctx_len, tc), lambda j, i: (0, 0, j)),
                  pl.BlockSpec((2, tc), lambda j, i: (0, j)),
                  pl.BlockSpec(fwd.shape, lambda j, i: (0, 0)),
                  pl.BlockSpec(inv.shape, lambda j, i: (0, 0))],
        out_specs=pl.BlockSpec((None, ctx_len, tc), lambda j, i: (i, 0, j)),
        out_shape=jax.ShapeDtypeStruct((b, ctx_len, c), F32),
        compiler_params=_params("parallel", "parallel"),
        name="hy_ctx",
    )(zc, zc, zc, taps, bias, fwd, inv)


def _hyena(zh, hy, seq, ctx_len, tables):
    short_w, short_b, w1, b1, w2, b2, freq, w3, bias = hy
    b, t, c3 = zh.shape
    c = c3 // 3
    f_a, f_c, g, gh = tables
    n2 = LANES
    zc = _time_mix(_short_conv_kernel, zh, jnp.concatenate([short_w, short_b[None]], axis=0), seq, "short_conv")

    hf = w2.shape[0]
    hp = LANES
    pad_c = lambda a: jnp.pad(a, ((0, 0), (0, hp - hf)))
    bands = jnp.pad(jnp.linspace(1e-4, HY_BANDS - 1, HY_BANDS, dtype=F32)[None, :], ((0, 0), (0, LANES - HY_BANDS)))
    w1c = jnp.pad(w1[1:1 + HY_BANDS], ((0, LANES - HY_BANDS), (0, hp - hf)))
    w1s = jnp.pad(w1[1 + HY_BANDS:], ((0, LANES - HY_BANDS), (0, hp - hf)))
    hid_args = (bands, pad_c(w1[0:1]), w1c, w1s, pad_c(b1[None]), jnp.pad(w2, ((0, hp - hf), (0, hp - hf))),
                pad_c(b2[None]), pad_c(freq))
    w3p = jnp.pad(w3, ((0, hp - hf), (0, 0)))
    max_decay = math.log(HY_TARGET) / HY_FAST_PCT
    min_decay = math.log(HY_TARGET) / HY_SLOW_PCT
    delta = jnp.abs(jnp.linspace(min_decay, max_decay, c, dtype=F32))[None, :]

    taps_l = _hy_taps(_hy_hidden(seq, *hid_args), w3p, delta, c)
    taps_c = _hy_taps(_hy_hidden(ctx_len, *hid_args), w3p, delta, c)

    as_split = lambda a: jnp.swapaxes(a.reshape(a.shape[0], seq // n2, n2, a.shape[-1]), 1, 2)
    tr, ti = _fft_a(as_split(taps_l), _split(0), c, n2, f_a)
    kr, ki = _fft_spec(tr, ti, g)

    zct = as_split(zc[:, :seq])
    bias3 = bias.reshape(2, 1, c)
    ar, ai = _fft_a(zct, _split(2), c, n2, f_a)
    zr, zi = _fft_mid(ar, ai, g, gh, kr, ki, 0)
    u1 = _fft_c(zr, zi, f_c, zct, _split(2), zct, _split(0), bias3, 0)
    ar, ai = _fft_a(u1, _merged(1, 0), c, n2, f_a)
    zr, zi = _fft_mid(ar, ai, g, gh, kr, ki, 1)
    u2 = _fft_c(zr, zi, f_c, u1, _merged(1, 0), zct, _split(1), bias3, 1)
    out_c = _hy_ctx(zc, taps_c, bias, seq, ctx_len, c)
    return jnp.concatenate([u2.reshape(b, seq, c), out_c], axis=1)


def _head_sum(x, n_heads):
    fold = x[:, :LANES]
    for s in range(LANES, x.shape[1], LANES):
        fold = fold + x[:, s:s + LANES]
    r = lax.broadcasted_iota(jnp.int32, (LANES, LANES), 0) & (n_heads - 1)
    c = lax.broadcasted_iota(jnp.int32, (LANES, LANES), 1) & (n_heads - 1)
    per_head = _dot_f32(fold, jnp.where(r == c, 1.0, 0.0))
    return jnp.concatenate([per_head] * (x.shape[1] // LANES), axis=1)


def _rw_feat_kernel(r_ref, k_ref, v_ref, zl_ref, wup_ref, aup_ref, gup_ref, w0_ref, a0_ref, kkp_ref, kap_ref,
                    rk_ref, kk_o, g_o, bonus_o, w_o, kd_o, kka_o, *, n_heads):
    r, k, v = r_ref[...], k_ref[...], v_ref[...]
    zl = zl_ref[...]
    wl, al, gl = zl[:, :LANES], zl[:, LANES:2 * LANES], zl[:, 2 * LANES:]
    kq = k * kkp_ref[...]
    kk = kq * lax.rsqrt(_head_sum(kq * kq, n_heads) + 1e-12)
    kk_o[...] = kk
    g_o[...] = _dot(_sigmoid(gl), gup_ref[...])
    twl = jnp.tanh(wl)
    bonus = jnp.zeros_like(v)
    for d in range(2):
        w = jnp.exp(-RW_DECAY_SCALE * _sigmoid(w0_ref[d:d + 1, :] + _dot_f32(twl, wup_ref[d])))
        a = _sigmoid(a0_ref[d:d + 1, :] + _dot_f32(al, aup_ref[d]))
        kd = k * (1.0 + (a - 1.0) * kap_ref[...])
        w_o[d] = w
        kd_o[d] = kd
        kka_o[d] = kk * a
        bonus = bonus + _head_sum(r * kd * rk_ref[...], n_heads) * v
    bonus_o[...] = bonus


def _rw_features(zs, zl, w_up, a_up, g_up, w0, a0, k_k, k_a, r_k, tr):
    b, t, c3 = zs.shape
    c = c3 // 3
    tok = lambda g: pl.BlockSpec((None, tr, c), lambda i, j: (i, j, g))
    full = lambda a: pl.BlockSpec(a.shape, lambda i, j: (0,) * a.ndim)
    two = pl.BlockSpec((2, None, tr, c), lambda i, j: (0, i, j, 0))
    s1 = jax.ShapeDtypeStruct((b, t, c), F32)
    s2 = jax.ShapeDtypeStruct((2, b, t, c), F32)
    consts = (w_up, a_up, g_up, w0, a0, k_k, k_a, r_k)
    return pl.pallas_call(
        functools.partial(_rw_feat_kernel, n_heads=c // RW_HEAD),
        grid=(b, t // tr),
        in_specs=[tok(0), tok(1), tok(2), pl.BlockSpec((None, tr, zl.shape[2]), lambda i, j: (i, j, 0))]
        + [full(a) for a in consts],
        out_specs=[tok(0), tok(0), tok(0), two, two, two],
        out_shape=[s1, s1, s1, s2, s2, s2],
        compiler_params=_params("parallel", "parallel"),
        name="rw_features",
    )(zs, zs, zs, zl, *consts)


RW_STEPS = 32
RW_PARTIALS = 4
RW_UNROLL = 2


def _replicated_row(ref, o, b, r):
    return ref[o, b, pl.ds(r, SUBLANES, stride=0), :]


def _rw_scan_kernel(*refs, reverse, n_heads, accumulate):
    if accumulate:
        r_ref, w_ref, kd_ref, v_ref, kk_ref, kka_ref, yin_ref, y_ref, s_ref, rows_ref = refs
    else:
        r_ref, w_ref, kd_ref, v_ref, kk_ref, kka_ref, y_ref, s_ref, rows_ref = refs
    n_batch, n_tiles = s_ref.shape[:2]
    steps = v_ref.shape[1]

    @pl.when(pl.program_id(0) == 0)
    def _():
        s_ref[...] = jnp.zeros_like(s_ref)

    operands = (kk_ref, kka_ref, kd_ref, w_ref, r_ref)
    for o, ref in enumerate(operands):
        for b in range(n_batch):
            for j in range(n_tiles):
                rows_ref[o, b, pl.ds(j, steps, stride=n_tiles), :] = ref[b, :, j * LANES:(j + 1) * LANES]
    KK, KKA, KD, W, R = range(len(operands))

    def total(parts):
        while len(parts) > 1:
            parts = [a + b for a, b in zip(parts[::2], parts[1::2])]
        x = parts[0].reshape(RW_HEAD, LANES)
        shift = LANES // 2
        while shift >= n_heads:
            x = x + pltpu.roll(x, shift, 1)
            shift //= 2
        return x.reshape(parts[0].shape)

    def accumulate_into(parts, j, term):
        parts[j % RW_PARTIALS] = term if parts[j % RW_PARTIALS] is None else parts[j % RW_PARTIALS] + term

    def one_step(b, t):
        row = lambda o, j: _replicated_row(rows_ref, o, b, t * n_tiles + j)[None]
        parts = [None] * RW_PARTIALS
        for j in range(n_tiles):
            accumulate_into(parts, j, s_ref[b, j] * row(KK, j))
        removed = total(parts)
        v = v_ref[b, t].reshape(removed.shape)
        parts = [None] * RW_PARTIALS
        for j in range(n_tiles):
            s = s_ref[b, j] * row(W, j) - removed * row(KKA, j) + v * row(KD, j)
            s_ref[b, j] = s
            accumulate_into(parts, j, s * row(R, j))
        y = total(parts).reshape(v_ref.shape[2:])
        y_ref[b, t] = (yin_ref[b, t] + y) if accumulate else y

    def group(gi, carry):
        for n in range(RW_UNROLL):
            i = gi * RW_UNROLL + n
            t = (steps - 1 - i) if reverse else i
            for b in range(n_batch):
                one_step(b, t)
        return carry

    lax.fori_loop(0, steps // RW_UNROLL, group, 0)


def _segment_order(n_lat, n_ctx, reverse):
    if reverse:
        return lambda j: n_lat + n_ctx - 1 - j
    return lambda j: jnp.where(j < n_ctx, n_lat + j, j - n_ctx)


def _rw_scan(zs, w, kd, vt, kk, kka, seq, direction, y_prev):
    b, t, c = kk.shape
    reverse = direction == 1
    order = _segment_order(seq // RW_STEPS, (t - seq) // RW_STEPS, reverse)
    tok = pl.BlockSpec((b, RW_STEPS, c), lambda j: (0, order(j), 0))
    dtok = pl.BlockSpec((None, b, RW_STEPS, c), lambda j: (direction, 0, order(j), 0))
    val = pl.BlockSpec((b, RW_STEPS, RW_HEAD, LANES), lambda j: (0, order(j), 0, 0))
    accumulate = y_prev is not None
    args = (zs, w, kd, vt, kk, kka) + ((y_prev,) if accumulate else ())
    n_tiles = c // LANES
    return pl.pallas_call(
        functools.partial(_rw_scan_kernel, reverse=reverse, n_heads=c // RW_HEAD, accumulate=accumulate),
        grid=(t // RW_STEPS,),
        in_specs=[tok, dtok, dtok, val, tok, dtok] + ([val] if accumulate else []),
        out_specs=val,
        out_shape=jax.ShapeDtypeStruct(vt.shape, F32),
        scratch_shapes=[pltpu.VMEM((b, n_tiles, RW_HEAD // SUBLANES, SUBLANES, LANES), F32),
                        pltpu.VMEM((5, b, RW_STEPS * n_tiles, LANES), F32)],
        input_output_aliases={6: 0} if accumulate else {},
        compiler_params=_params("arbitrary"),
        name="rw_scan",
    )(*args)


def _rw_out_kernel(y_ref, bonus_ref, g_ref, lnw_ref, lnb_ref, o_ref, *, n_heads):
    y = y_ref[...]
    mean = _head_sum(y, n_heads) * (1.0 / RW_HEAD)
    yc = y - mean
    var = _head_sum(yc * yc, n_heads) * (1.0 / RW_HEAD)
    yn = yc * lax.rsqrt(var + RW_GN_EPS) * lnw_ref[...] + lnb_ref[...]
    o_ref[...] = ((yn + bonus_ref[...]) * g_ref[...]).astype(o_ref.dtype)


def _rw_out(y, bonus, g, ln_w, ln_b, tr):
    b, t, c = y.shape
    tok = pl.BlockSpec((None, tr, c), lambda i, j: (i, j, 0))
    vec = pl.BlockSpec((1, c), lambda i, j: (0, 0))
    return pl.pallas_call(
        functools.partial(_rw_out_kernel, n_heads=c // RW_HEAD),
        grid=(b, t // tr),
        in_specs=[tok, tok, tok, vec, vec],
        out_specs=tok,
        out_shape=jax.ShapeDtypeStruct((b, t, c), BF16),
        compiler_params=_params("parallel", "parallel"),
        name="rw_out",
    )(y, bonus, g, ln_w, ln_b)


def _pad_rows(a, rows):
    return jnp.pad(a, [(0, 0)] * (a.ndim - 2) + [(0, rows - a.shape[-2]), (0, 0)])


def _rw_perm(c):
    return jnp.arange(c).reshape(c // RW_HEAD, RW_HEAD).T.reshape(-1)


def _rwkv(zs_raw, zl_raw, rw, seq, tr):
    mu, w0, w_up, a0, a_up, g_up, k_k, k_a, r_k, ln_w, ln_b = rw
    b, t, _ = zs_raw.shape
    c = k_k.shape[0]
    n_heads = c // RW_HEAD
    perm = _rw_perm(c)
    pc = lambda a: jnp.take(a, perm, axis=-1)
    lw, la = w_up.shape[1], a_up.shape[1]
    mu_l = mu[3 * c:]
    pad_to = lambda a: jnp.pad(a, (0, LANES - a.shape[0]))
    mu_lora = jnp.concatenate([pad_to(mu_l[:lw]), pad_to(mu_l[lw:lw + la]), mu_l[lw + la:]])
    mu_main = pc(mu[:3 * c].reshape(3, c)).reshape(1, 3 * c)
    zs = _time_mix(_token_shift_kernel, zs_raw, mu_main, seq, "token_shift")
    zl = _time_mix(_token_shift_kernel, zl_raw, mu_lora[None, :], seq, "token_shift")
    kk, g, bonus, w, kd, kka = _rw_features(
        zs, zl, _pad_rows(pc(w_up), LANES), _pad_rows(pc(a_up), LANES), pc(g_up).astype(BF16), pc(w0), pc(a0),
        pc(k_k)[None], pc(k_a)[None], pc(r_k.reshape(c))[None], min(tr, 128))

    vt = jnp.tile(zs[..., 2 * c:].reshape(b, t, RW_HEAD, n_heads), (1, 1, 1, LANES // n_heads))
    y = _rw_scan(zs, w, kd, vt, kk, kka, seq, 0, None)
    y = _rw_scan(zs, w, kd, vt, kk, kka, seq, 1, y)
    y = y[..., :n_heads].reshape(b, t, c)
    return _rw_out(y, bonus, g, pc(ln_w)[None], pc(ln_b)[None], tr)


def _hg_bounds_kernel(x_ref, o_ref):
    x = x_ref[...]
    e = jnp.exp(x - jnp.max(x, axis=0, keepdims=True))
    p = e / jnp.sum(e, axis=0, keepdims=True)
    run = jnp.zeros_like(p[0:1])
    o_ref[0:1, :] = run
    for i in range(1, x.shape[0]):
        run = run + p[i:i + 1]
        o_ref[i:i + 1, :] = run


def _hg_bounds(lower):
    return pl.pallas_call(
        _hg_bounds_kernel,
        out_shape=jax.ShapeDtypeStruct(lower.shape, F32),
        name="hg_bounds",
    )(lower)


HG_HEADS_PER_STEP = 16


def _hg_scan_kernel(q_ref, f_ref, i_ref, lb_ref, o_ref, s_ref, *, reverse):
    cs = q_ref.shape[0]
    n_heads = s_ref.shape[0]

    @pl.when(pl.program_id(2) == 0)
    def _():
        s_ref[...] = jnp.zeros_like(s_ref)

    width = q_ref.shape[1]
    row = lax.broadcasted_iota(jnp.int32, (cs, cs), 0)
    col = lax.broadcasted_iota(jnp.int32, (cs, cs), 1)
    row_w = lax.broadcasted_iota(jnp.int32, (cs, width), 0)
    inclusive = lambda upto: jnp.where((col >= upto) if reverse else (col <= upto), 1.0, 0.0)
    sums = [inclusive(row)]
    levels = []
    hs = cs // 2
    while hs >= 1:
        blk = 2 * hs
        boundary = (row & ~(blk - 1)) + hs
        sums.append(inclusive(boundary if reverse else boundary - 1))
        levels.append(((row & ~(blk - 1)) == (col & ~(blk - 1)), (row_w & (blk - 1)) >= hs))
        hs //= 2

    heads = [slice(h * HG_HEAD, (h + 1) * HG_HEAD) for h in range(n_heads)]
    q, v, lb = q_ref[...], i_ref[...], lb_ref[...]
    f = lb + (1.0 - lb) * _sigmoid(f_ref[...])
    k = 1.0 - f
    bb = _dot_f32(jnp.concatenate(sums, axis=0), jnp.log(f))
    b = bb[:cs]
    qk = q * k
    att = [jnp.where(row == col, jnp.sum(qk[:, sl], axis=-1, keepdims=True), 0.0) for sl in heads]
    for l, (same, late) in enumerate(levels):
        ref_b = bb[(l + 1) * cs:(l + 2) * cs]
        q_side, k_side = (jnp.logical_not(late), late) if reverse else (late, jnp.logical_not(late))
        qt = (q * jnp.exp(jnp.where(q_side, b - ref_b, -jnp.inf))).astype(BF16)
        kt = (k * jnp.exp(jnp.where(k_side, ref_b - b, -jnp.inf))).astype(BF16)
        for h, sl in enumerate(heads):
            att[h] = att[h] + jnp.where(same, _dot_nt(qt[:, sl], kt[:, sl]), 0.0)
    b_end = b[0:1, :] if reverse else b[cs - 1:cs, :]
    qe = (q * jnp.exp(b)).astype(BF16)
    ke = (k * jnp.exp(b_end - b)).astype(BF16)
    decay = jnp.exp(b_end)
    vb = v.astype(BF16)
    for h, sl in enumerate(heads):
        st = s_ref[h]
        o_ref[:, sl] = _dot(att[h], vb[:, sl]) + _dot_nt(qe[:, sl], st)
        s_ref[h] = decay[:, sl] * st + _dot_tn(vb[:, sl], ke[:, sl])


def _hg_scan(p, lb, c, seq, direction):
    b, t, _ = p.shape
    wl = min(HG_HEADS_PER_STEP * HG_HEAD, c)
    ng = c // wl
    reverse = direction == 1
    order = _segment_order(seq // HG_CHUNK, (t - seq) // HG_CHUNK, reverse)
    blk = lambda g: pl.BlockSpec((None, HG_CHUNK, wl), lambda i, n, j: (i, order(j), g * ng + n))
    return pl.pallas_call(
        functools.partial(_hg_scan_kernel, reverse=reverse),
        grid=(b, ng, t // HG_CHUNK),
        in_specs=[blk(0), blk(1 + direction), blk(3), pl.BlockSpec((1, wl), lambda i, n, j: (0, n))],
        out_specs=blk(0),
        out_shape=jax.ShapeDtypeStruct((b, t, c), F32),
        scratch_shapes=[pltpu.VMEM((wl // HG_HEAD, HG_HEAD, HG_HEAD), F32)],
        compiler_params=_params("parallel", "parallel", "arbitrary"),
        name="hg_scan",
    )(p, p, p, lb)


def _hg_out_kernel(o0_ref, o1_ref, g_ref, w_ref, out_ref):
    o = o0_ref[...] + o1_ref[...]
    g = g_ref[...]
    for s in range(0, o.shape[1], HG_HEAD):
        oh = o[:, s:s + HG_HEAD]
        gh = g[:, s:s + HG_HEAD]
        out_ref[:, s:s + HG_HEAD] = (_rms(oh) * w_ref[...] * (gh * _sigmoid(gh))).astype(out_ref.dtype)


def _hg_out(o0, o1, p, norm_w, tr):
    b, t, c = o0.shape
    tc = _pick(c, (512, 256, 128))
    nct = c // tc
    tok = pl.BlockSpec((None, tr, tc), lambda i, j, n: (i, j, n))
    return pl.pallas_call(
        _hg_out_kernel,
        grid=(b, t // tr, nct),
        in_specs=[tok, tok, pl.BlockSpec((None, tr, tc), lambda i, j, n: (i, j, 4 * nct + n)),
                  pl.BlockSpec((1, HG_HEAD), lambda i, j, n: (0, 0))],
        out_specs=tok,
        out_shape=jax.ShapeDtypeStruct((b, t, c), BF16),
        compiler_params=_params("parallel", "parallel", "parallel"),
        name="hg_out",
    )(o0, o1, p, norm_w)


def _rope_tables(seq):
    t = jnp.arange(seq)
    axis = AT_HEAD // 2
    inv = ROPE_THETA ** (-jnp.arange(0, axis, 2, dtype=F32) / axis)
    ang = jnp.concatenate([(t // GRID_W)[:, None] * inv, (t % GRID_W)[:, None] * inv], axis=-1)
    sign = jnp.tile(jnp.array([-1.0, 1.0], F32), AT_HEAD // 2)
    return jnp.repeat(jnp.cos(ang), 2, axis=-1), jnp.repeat(jnp.sin(ang), 2, axis=-1) * sign


def _at_prep_kernel(q_ref, k_ref, qw_ref, kw_ref, cos_ref, sin_ref, qo_ref, ko_ref, *, n_lat):
    is_lat = pl.program_id(1) < n_lat
    cos, sin = cos_ref[...], sin_ref[...]
    even = (lax.broadcasted_iota(jnp.int32, cos.shape, 1) & 1) == 0

    def head(x, w):
        y = _rms(x) * w
        partner = jnp.where(even, pltpu.roll(y, AT_HEAD - 1, 1), pltpu.roll(y, 1, 1))
        return jnp.where(is_lat, y * cos + partner * sin, y)

    scale = AT_HEAD ** -0.5
    for s in range(0, qo_ref.shape[1], AT_HEAD):
        qo_ref[:, s:s + AT_HEAD] = (head(q_ref[:, s:s + AT_HEAD], qw_ref[...]) * scale).astype(qo_ref.dtype)
    for s in range(0, ko_ref.shape[1], AT_HEAD):
        ko_ref[:, s:s + AT_HEAD] = head(k_ref[:, s:s + AT_HEAD], kw_ref[...]).astype(ko_ref.dtype)


def _at_prep(p, c, q_norm, k_norm, cos, sin, seq, tr):
    b, t, _ = p.shape
    kvw = AT_KV_HEADS * AT_HEAD
    n_lat = seq // tr
    tab = pl.BlockSpec((tr, AT_HEAD), lambda i, j: (jnp.minimum(j, n_lat - 1), 0))
    vec = pl.BlockSpec((1, AT_HEAD), lambda i, j: (0, 0))
    return pl.pallas_call(
        functools.partial(_at_prep_kernel, n_lat=n_lat),
        grid=(b, t // tr),
        in_specs=[pl.BlockSpec((None, tr, c), lambda i, j: (i, j, 5)),
                  pl.BlockSpec((None, tr, kvw), lambda i, j: (i, j, 6 * c // kvw)),
                  vec, vec, tab, tab],
        out_specs=[pl.BlockSpec((None, tr, c), lambda i, j: (i, j, 0)),
                   pl.BlockSpec((None, tr, kvw), lambda i, j: (i, j, 0))],
        out_shape=[jax.ShapeDtypeStruct((b, t, c), BF16), jax.ShapeDtypeStruct((b, t, kvw), BF16)],
        compiler_params=_params("parallel", "parallel"),
        name="at_prep",
    )(p, p, q_norm, k_norm, cos, sin)


AT_Q_BLOCK = 256
AT_AHEAD = 2


def _flash_kernel(q_ref, k_ref, v_ref, o_ref, m_ref, l_ref, acc_ref):
    kv = pl.program_id(3)

    @pl.when(kv == 0)
    def _():
        m_ref[...] = jnp.full_like(m_ref, -jnp.inf)
        l_ref[...] = jnp.zeros_like(l_ref)
        acc_ref[...] = jnp.zeros_like(acc_ref)

    k = k_ref[...]
    v = v_ref[...].astype(BF16)
    group, tq, _ = m_ref.shape
    qb = min(AT_Q_BLOCK, tq)
    blocks = [(r, i) for r in range(group) for i in range(tq // qb)]

    def scores(r, i):
        return _dot_nt(q_ref[i * qb:(i + 1) * qb, r * AT_HEAD:(r + 1) * AT_HEAD], k)

    ahead = [scores(*blk) for blk in blocks[:AT_AHEAD]]
    for n, (r, i) in enumerate(blocks):
        s = ahead.pop(0)
        if n + AT_AHEAD < len(blocks):
            ahead.append(scores(*blocks[n + AT_AHEAD]))
        rows = slice(i * qb, (i + 1) * qb)
        m_prev = m_ref[r, rows, :]
        m_cur = jnp.maximum(m_prev, jnp.max(s, axis=-1, keepdims=True))
        alpha = jnp.exp(m_prev - m_cur)
        p = jnp.exp(s - m_cur[:, 0:1])
        l_ref[r, rows, :] = alpha * l_ref[r, rows, :] + jnp.sum(p, axis=-1, keepdims=True)
        acc_ref[r, rows, :] = alpha * acc_ref[r, rows, :] + _dot(p, v)
        m_ref[r, rows, :] = m_cur

    @pl.when(kv == pl.num_programs(3) - 1)
    def _():
        for r in range(group):
            o_ref[:, r * AT_HEAD:(r + 1) * AT_HEAD] = (acc_ref[r] / l_ref[r]).astype(o_ref.dtype)


def _flash(qn, kn, p, c, q_blk0, tq, nq, kv_blk0, tk, nk):
    b = qn.shape[0]
    group = c // AT_HEAD // AT_KV_HEADS
    gw = group * AT_HEAD
    v_col0 = (6 * c + AT_KV_HEADS * AT_HEAD) // AT_HEAD
    return pl.pallas_call(
        _flash_kernel,
        grid=(b, AT_KV_HEADS, nq, nk),
        in_specs=[pl.BlockSpec((None, tq, gw), lambda i, g, a, n: (i, q_blk0 + a, g)),
                  pl.BlockSpec((None, tk, AT_HEAD), lambda i, g, a, n: (i, kv_blk0 + n, g)),
                  pl.BlockSpec((None, tk, AT_HEAD), lambda i, g, a, n: (i, kv_blk0 + n, v_col0 + g))],
        out_specs=pl.BlockSpec((None, tq, gw), lambda i, g, a, n: (i, a, g)),
        out_shape=jax.ShapeDtypeStruct((b, nq * tq, c), BF16),
        scratch_shapes=[pltpu.VMEM((group, tq, AT_HEAD), F32)] * 3,
        compiler_params=_params("parallel", "parallel", "parallel", "arbitrary"),
        name="flash",
    )(qn, kn, p)


def kernel(x, c, ctx, c_ctx, ada_down, ada_up, ada_bias, norm_g, ffn_w1, ffn_w3, ffn_w2, ev_w_in, ev_w_out, hy_short_w, hy_short_b, hy_pe_w1, hy_pe_b1, hy_pe_w2, hy_pe_b2, hy_sin_freq, hy_pe_w3, hy_bias, rw_mu, rw_w0, rw_w_up, rw_a0, rw_a_up, rw_g_up, rw_k_k, rw_k_a, rw_r_k, rw_ln_w, rw_ln_b, od_w_in, od_w_out, hg_lower_bounds, hg_norm_g, at_q_norm, at_k_norm):
    bsz, seq, d = x.shape
    ctx_len = ctx.shape[1]
    t = seq + ctx_len
    depth = ada_down.shape[0]
    half = d // 2
    tr = min(256, ctx_len)
    assert seq % ctx_len == 0 and ctx_len % LANES == 0 and bsz < SUBLANES

    xs = jnp.concatenate([x, ctx], axis=1)
    cond = jnp.concatenate([c, c_ctx[None], jnp.zeros((SUBLANES - bsz - 1, d), F32)], axis=0)
    m = _adaln(cond, ada_down, ada_up, ada_bias).reshape(depth, SUBLANES, N_MOD, d)
    mods = jnp.stack([m[:, :bsz], jnp.broadcast_to(m[:, bsz:bsz + 1], (depth, bsz, N_MOD, d))], axis=2)

    n1 = 2 * seq // LANES
    tables = _fft_tables(n1, LANES)
    cos, sin = _rope_tables(seq)
    lb_all = _hg_bounds(hg_lower_bounds)

    for l in range(depth):
        h = _norm_mod(xs, norm_g, mods, l, seq, tr, 0, 0, 1).reshape(bsz * t, d)
        if l % 2 == 0:
            e = l // 2
            w_in = ev_w_in[e]
            n_hy = 3 * half
            lw, la = rw_w_up.shape[2], rw_a_up.shape[2]
            w_lo = w_in[:, 2 * n_hy:]
            pad_c = lambda a: jnp.pad(a, ((0, 0), (0, LANES - a.shape[1])))
            w_lora = jnp.concatenate([pad_c(w_lo[:, :lw]), pad_c(w_lo[:, lw:lw + la]), w_lo[:, lw + la:]], axis=1)
            n_rw_heads = half // RW_HEAD
            w_rkv = w_in[:, n_hy:2 * n_hy].reshape(d, 3, n_rw_heads, RW_HEAD).swapaxes(2, 3).reshape(d, n_hy)
            w_main = jnp.concatenate([w_in[:, :n_hy], w_rkv], axis=1).astype(BF16)
            zh = _mm([h], w_main, 0, n_hy, F32).reshape(bsz, t, n_hy)
            zs = _mm([h], w_main, n_hy, n_hy, F32).reshape(bsz, t, n_hy)
            zl = _mm([h], w_lora.astype(BF16), 0, w_lora.shape[1], F32).reshape(bsz, t, -1)
            hy = (hy_short_w[e], hy_short_b[e], hy_pe_w1[e], hy_pe_b1[e], hy_pe_w2[e], hy_pe_b2[e],
                  hy_sin_freq[e], hy_pe_w3[e], hy_bias[e])
            rw = (rw_mu[e], rw_w0[e], rw_w_up[e], rw_a0[e], rw_a_up[e], rw_g_up[e], rw_k_k[e],
                  rw_k_a[e], rw_r_k[e], rw_ln_w[e], rw_ln_b[e])
            mix_a = _hyena(zh, hy, seq, ctx_len, tables).astype(BF16)
            mix_b = _rwkv(zs, zl, rw, seq, tr)
            w_rw_rows = ev_w_out[e][half:].reshape(n_rw_heads, RW_HEAD, d).swapaxes(0, 1).reshape(half, d)
            w_out = jnp.concatenate([ev_w_out[e][:half], w_rw_rows], axis=0)
        else:
            o = l // 2
            p = _mm([h], od_w_in[o].astype(BF16), 0, od_w_in.shape[2], F32).reshape(bsz, t, -1)
            lb = lb_all[l][None, :]
            o0 = _hg_scan(p, lb, half, seq, 0)
            o1 = _hg_scan(p, lb, half, seq, 1)
            mix_a = _hg_out(o0, o1, p, hg_norm_g[o][None], tr)
            qn, kn = _at_prep(p, half, at_q_norm[o][None], at_k_norm[o][None], cos, sin, seq, tr)
            tq = _pick(seq, (512, 256, 128))
            tk = _pick(t, (768, 512, 256, 128))
            at_l = _flash(qn, kn, p, half, 0, tq, seq // tq, 0, tk, t // tk)
            at_c = _flash(qn, kn, p, half, seq // ctx_len, ctx_len, 1, seq // ctx_len, ctx_len, 1)
            mix_b = jnp.concatenate([at_l, at_c], axis=1)
            w_out = od_w_out[o]
        y = _mm([mix_a.reshape(bsz * t, half), mix_b.reshape(bsz * t, half)], w_out.astype(BF16), 0, d, F32)
        xs = _resid(xs, y, norm_g, mods, l, seq, tr, 1, 2)
        h = _norm_mod(xs, norm_g, mods, l, seq, tr, 2, 3, 4).reshape(bsz * t, d)
        hid = _swiglu_up(h, ffn_w1[l].astype(BF16), ffn_w3[l].astype(BF16))
        y = _mm_ksplit(hid, ffn_w2[l].astype(BF16))
        xs = _resid(xs, y, norm_g, mods, l, seq, tr, 3, 5)
    return xs[:, :seq]
```

```python
import functools
import math

import numpy as np
import jax
import jax.numpy as jnp
from jax import lax
from jax.experimental import pallas as pl
from jax.experimental.pallas import tpu as pltpu

F32 = jnp.float32
BF16 = jnp.bfloat16
HIGHEST = lax.Precision.HIGHEST

NORM_EPS = 1e-6
N_MOD = 6
GRID_W = 64
HY_BANDS = 16
HY_TARGET = 1e-2
HY_FAST_PCT = 0.3
HY_SLOW_PCT = 1.5
RW_HEAD = 64
RW_DECAY_SCALE = math.exp(-0.5)
RW_GN_EPS = 64e-5
HG_HEAD = 128
HG_CHUNK = 64
AT_HEAD = 128
AT_KV_HEADS = 4
ROPE_THETA = 10000.0

LANES = 128
SUBLANES = 8
VMEM_LIMIT = 56 << 20


def _params(*sem):
    return pltpu.CompilerParams(dimension_semantics=sem, vmem_limit_bytes=VMEM_LIMIT)


def _dot(a, b):
    return jnp.dot(a.astype(BF16), b.astype(BF16), preferred_element_type=F32)


def _dot_f32(a, b):
    return jnp.dot(a, b, precision=HIGHEST, preferred_element_type=F32)


def _dot_nt(a, b, exact=False):
    dn = (((1,), (1,)), ((), ()))
    if exact:
        return lax.dot_general(a, b, dn, precision=HIGHEST, preferred_element_type=F32)
    return lax.dot_general(a.astype(BF16), b.astype(BF16), dn, preferred_element_type=F32)


def _dot_tn(a, b, exact=False):
    dn = (((0,), (0,)), ((), ()))
    if exact:
        return lax.dot_general(a, b, dn, precision=HIGHEST, preferred_element_type=F32)
    return lax.dot_general(a.astype(BF16), b.astype(BF16), dn, preferred_element_type=F32)


def _sigmoid(x):
    return jax.nn.sigmoid(x)


def _pick(n, prefs):
    for p in prefs:
        if n % p == 0:
            return p
    return n


def _adaln_kernel(cond_ref, down_ref, up_ref, bias_ref, o_ref):
    cnd = cond_ref[...]
    t = _dot_f32(cnd * _sigmoid(cnd), down_ref[...])
    o_ref[...] = _dot_f32(t, up_ref[...]) + bias_ref[...]


def _adaln(cond8, down, up, bias):
    depth, d, r = down.shape
    n = up.shape[2]
    tn = _pick(n, (2048, 1024, 512, 256, 128))
    return pl.pallas_call(
        _adaln_kernel,
        grid=(depth, n // tn),
        in_specs=[
            pl.BlockSpec((SUBLANES, d), lambda l, j: (0, 0)),
            pl.BlockSpec((None, d, r), lambda l, j: (l, 0, 0)),
            pl.BlockSpec((None, r, tn), lambda l, j: (l, 0, j)),
            pl.BlockSpec((None, 1, tn), lambda l, j: (l, 0, j)),
        ],
        out_specs=pl.BlockSpec((None, SUBLANES, tn), lambda l, j: (l, 0, j)),
        out_shape=jax.ShapeDtypeStruct((depth, SUBLANES, n), F32),
        compiler_params=_params("arbitrary", "arbitrary"),
        name="adaln",
    )(cond8, down, up, bias.reshape(depth, 1, n))


def _rms(x):
    return x * lax.rsqrt(jnp.mean(x * x, axis=-1, keepdims=True) + NORM_EPS)


def _norm_mod_kernel(x_ref, g_ref, mod_ref, o_ref, *, g_row, shift_row, scale_row):
    y = _rms(x_ref[...]) * g_ref[g_row:g_row + 1, :]
    y = y * (1.0 + mod_ref[scale_row:scale_row + 1, :]) + mod_ref[shift_row:shift_row + 1, :]
    o_ref[...] = y.astype(o_ref.dtype)


def _resid_kernel(x_ref, y_ref, g_ref, mod_ref, o_ref, *, g_row, gate_row):
    yn = _rms(y_ref[...]) * g_ref[g_row:g_row + 1, :]
    o_ref[...] = x_ref[...] + mod_ref[gate_row:gate_row + 1, :] * yn


def _row_specs(seq, tr, d, layer):
    n_lat = seq // tr
    tok = pl.BlockSpec((None, tr, d), lambda b, i: (b, i, 0))
    g = pl.BlockSpec((None, 4, d), lambda b, i: (layer, 0, 0))
    mod = pl.BlockSpec((None, None, None, N_MOD, d),
                       lambda b, i: (layer, b, jnp.where(i < n_lat, 0, 1), 0, 0))
    return tok, g, mod


def _norm_mod(xs, norm_g, mods, layer, seq, tr, g_row, shift_row, scale_row):
    b, t, d = xs.shape
    tok, g, mod = _row_specs(seq, tr, d, layer)
    return pl.pallas_call(
        functools.partial(_norm_mod_kernel, g_row=g_row, shift_row=shift_row, scale_row=scale_row),
        grid=(b, t // tr),
        in_specs=[tok, g, mod],
        out_specs=tok,
        out_shape=jax.ShapeDtypeStruct((b, t, d), BF16),
        compiler_params=_params("parallel", "parallel"),
        name="norm_mod",
    )(xs, norm_g, mods)


def _resid(xs, y, norm_g, mods, layer, seq, tr, g_row, gate_row):
    b, t, d = xs.shape
    tok, g, mod = _row_specs(seq, tr, d, layer)
    return pl.pallas_call(
        functools.partial(_resid_kernel, g_row=g_row, gate_row=gate_row),
        grid=(b, t // tr),
        in_specs=[tok, tok, g, mod],
        out_specs=tok,
        out_shape=jax.ShapeDtypeStruct((b, t, d), F32),
        compiler_params=_params("parallel", "parallel"),
        name="resid",
    )(xs, y.reshape(b, t, d), norm_g, mods)


def _mm_kernel(*refs, n_in):
    a_refs, w_refs, o_ref = refs[:n_in], refs[n_in:2 * n_in], refs[2 * n_in]
    acc = _dot(a_refs[0][...], w_refs[0][...])
    for a_ref, w_ref in zip(a_refs[1:], w_refs[1:]):
        acc = acc + _dot(a_ref[...], w_ref[...])
    o_ref[...] = acc.astype(o_ref.dtype)


def _mm(a_list, w, col0, ncols, out_dtype, tm=1536, tn=512):
    m = a_list[0].shape[0]
    tm = _pick(m, (tm, 512, 256, 128))
    tn = next(t for t in (tn, 256, 128) if ncols % t == 0 and col0 % t == 0)
    n_in = len(a_list)
    kw = a_list[0].shape[1]
    a_specs = [pl.BlockSpec((tm, kw), lambda i, j: (i, 0)) for _ in a_list]
    w_specs = [pl.BlockSpec((kw, tn), functools.partial(lambda i, j, r: (r, col0 // tn + j), r=r))
               for r in range(n_in)]
    return pl.pallas_call(
        functools.partial(_mm_kernel, n_in=n_in),
        grid=(m // tm, ncols // tn),
        in_specs=a_specs + w_specs,
        out_specs=pl.BlockSpec((tm, tn), lambda i, j: (i, j)),
        out_shape=jax.ShapeDtypeStruct((m, ncols), out_dtype),
        compiler_params=_params("parallel", "parallel"),
        name="proj",
    )(*a_list, *([w] * n_in))


def _swiglu_kernel(a_ref, w1_ref, w3_ref, o_ref):
    a = a_ref[...]
    g = _dot(a, w1_ref[...])
    u = _dot(a, w3_ref[...])
    o_ref[...] = (g * _sigmoid(g) * u).astype(o_ref.dtype)


def _swiglu_up(a, w1, w3, tm=1536, tn=256):
    m, k = a.shape
    n = w1.shape[1]
    tm = _pick(m, (tm, 512, 256, 128))
    tn = _pick(n, (tn, 128))
    return pl.pallas_call(
        _swiglu_kernel,
        grid=(m // tm, n // tn),
        in_specs=[pl.BlockSpec((tm, k), lambda i, j: (i, 0)),
                  pl.BlockSpec((k, tn), lambda i, j: (0, j)),
                  pl.BlockSpec((k, tn), lambda i, j: (0, j))],
        out_specs=pl.BlockSpec((tm, tn), lambda i, j: (i, j)),
        out_shape=jax.ShapeDtypeStruct((m, n), BF16),
        compiler_params=_params("parallel", "parallel"),
        name="swiglu_up",
    )(a, w1, w3)


def _mm_acc_kernel(a_ref, w_ref, o_ref, acc_ref):
    k = pl.program_id(2)

    @pl.when(k == 0)
    def _():
        acc_ref[...] = jnp.zeros_like(acc_ref)

    acc_ref[...] += _dot(a_ref[...], w_ref[...])

    @pl.when(k == pl.num_programs(2) - 1)
    def _():
        o_ref[...] = acc_ref[...]


def _mm_ksplit(a, w, tm=768, tn=1024):
    m, kdim = a.shape
    n = w.shape[1]
    tm = _pick(m, (tm, 512, 256, 128))
    tn = _pick(n, (tn, 512, 256, 128))
    tk = kdim // 2 if (kdim // 2) % LANES == 0 else kdim
    return pl.pallas_call(
        _mm_acc_kernel,
        grid=(m // tm, n // tn, kdim // tk),
        in_specs=[pl.BlockSpec((tm, tk), lambda i, j, k: (i, k)),
                  pl.BlockSpec((tk, tn), lambda i, j, k: (k, j))],
        out_specs=pl.BlockSpec((tm, tn), lambda i, j, k: (i, j)),
        out_shape=jax.ShapeDtypeStruct((m, n), F32),
        scratch_shapes=[pltpu.VMEM((tm, tn), F32)],
        compiler_params=_params("parallel", "parallel", "arbitrary"),
        name="ffn_down",
    )(a, w)


def _neighbours(x, seq):
    t = x.shape[0]
    row = lax.broadcasted_iota(jnp.int32, x.shape, 0)
    prev = jnp.where(row == 0, 0.0, jnp.where(row == seq, 0.0, pltpu.roll(x, 1, 0)))
    nxt = jnp.where(row == seq - 1, 0.0, jnp.where(row == t - 1, 0.0, pltpu.roll(x, t - 1, 0)))
    return prev, nxt


def _short_conv_kernel(z_ref, p_ref, o_ref, *, seq):
    z = z_ref[...]
    prev, nxt = _neighbours(z, seq)
    o_ref[...] = p_ref[0:1, :] * prev + p_ref[1:2, :] * z + p_ref[2:3, :] * nxt + p_ref[3:4, :]


def _token_shift_kernel(z_ref, p_ref, o_ref, *, seq):
    z = z_ref[...]
    prev, nxt = _neighbours(z, seq)
    o_ref[...] = z + p_ref[...] * (0.5 * (prev + nxt) - z)


def _time_mix(body, z, params, seq, name):
    b, t, cw = z.shape
    tc = LANES
    return pl.pallas_call(
        functools.partial(body, seq=seq),
        grid=(b, cw // tc),
        in_specs=[pl.BlockSpec((None, t, tc), lambda i, j: (i, 0, j)),
                  pl.BlockSpec((params.shape[0], tc), lambda i, j: (0, j))],
        out_specs=pl.BlockSpec((None, t, tc), lambda i, j: (i, 0, j)),
        out_shape=jax.ShapeDtypeStruct((b, t, cw), F32),
        compiler_params=_params("parallel", "parallel"),
        name=name,
    )(z, params)


def _hy_hidden_kernel(bands_ref, w1t_ref, w1c_ref, w1s_ref, b1_ref, w2_ref, b2_ref, freq_ref, o_ref, *, length):
    hp = o_ref.shape[1]
    pos_b = lax.broadcasted_iota(jnp.int32, (length, bands_ref.shape[1]), 0).astype(F32)
    ang = (2 * math.pi) * pos_b / length * bands_ref[...]
    t = lax.broadcasted_iota(jnp.int32, (length, hp), 0).astype(F32) / max(length - 1, 1)
    pre = (t * w1t_ref[...] + _dot_f32(jnp.cos(ang), w1c_ref[...]) + _dot_f32(-jnp.sin(ang), w1s_ref[...])
           + b1_ref[...])
    h = jnp.sin(freq_ref[0:1, :] * pre)
    o_ref[...] = jnp.sin(freq_ref[1:2, :] * (_dot_f32(h, w2_ref[...]) + b2_ref[...]))


def _hy_hidden(length, bands, w1t, w1c, w1s, b1, w2, b2, freq):
    hp = w2.shape[0]
    full = lambda a: pl.BlockSpec(a.shape, lambda i: (0,) * a.ndim)
    args = (bands, w1t, w1c, w1s, b1, w2, b2, freq)
    return pl.pallas_call(
        functools.partial(_hy_hidden_kernel, length=length),
        grid=(1,),
        in_specs=[full(a) for a in args],
        out_specs=pl.BlockSpec((length, hp), lambda i: (0, 0)),
        out_shape=jax.ShapeDtypeStruct((length, hp), F32),
        compiler_params=_params("arbitrary"),
        name="hy_hidden",
    )(*args)


def _hy_taps_kernel(h_ref, w3_ref, delta_ref, o_ref, *, length):
    t = lax.broadcasted_iota(jnp.int32, o_ref.shape, 0).astype(F32) / max(length - 1, 1)
    o_ref[...] = _dot_f32(h_ref[...], w3_ref[...]) * jnp.exp(-t * delta_ref[...])


def _hy_taps(hmid, w3p, delta, c):
    length, hp = hmid.shape
    tc = _pick(c, (256, 128))
    nct = c // tc
    return pl.pallas_call(
        functools.partial(_hy_taps_kernel, length=length),
        grid=(4, nct),
        in_specs=[pl.BlockSpec((length, hp), lambda g, j: (0, 0)),
                  pl.BlockSpec((hp, tc), lambda g, j: (0, g * nct + j)),
                  pl.BlockSpec((1, tc), lambda g, j: (0, j))],
        out_specs=pl.BlockSpec((None, length, tc), lambda g, j: (g, 0, j)),
        out_shape=jax.ShapeDtypeStruct((4, length, c), F32),
        compiler_params=_params("parallel", "parallel"),
        name="hy_taps",
    )(hmid, w3p, delta)


def _fft_tables(n1, n2):
    n = n1 * n2
    n1h = n1 // 2

    def cs(phase, period):
        ang = (phase % period).astype(F32) * (2 * math.pi / period)
        return jnp.cos(ang), jnp.sin(ang)

    k1 = jnp.arange(n1, dtype=jnp.int32)
    c, s = cs(k1[:, None] * jnp.arange(n1h, dtype=jnp.int32)[None, :], n1)
    f_a = jnp.concatenate([c, -s], axis=0)
    c, s = cs(jnp.arange(n1h, dtype=jnp.int32)[:, None] * k1[None, :], n1)
    f_c = jnp.concatenate([c, -s], axis=1) / n
    k2 = jnp.arange(n2, dtype=jnp.int32)
    freq = k1[:, None, None] + n1 * k2[None, :, None]
    c, s = cs(freq * k2[None, None, :], n)
    g = jnp.concatenate([jnp.concatenate([c, s], axis=2),
                         jnp.concatenate([-s, c], axis=2)], axis=1)
    ct, st = jnp.swapaxes(c, 1, 2), jnp.swapaxes(s, 1, 2)
    gh = jnp.concatenate([jnp.concatenate([ct, -st], axis=2),
                          jnp.concatenate([st, ct], axis=2)], axis=1)
    return f_a.astype(BF16), f_c.astype(BF16), g.astype(BF16), gh.astype(BF16)


def _fft_a_kernel(u_ref, f_ref, ar_ref, ai_ref):
    n1 = ar_ref.shape[0]
    res = _dot(f_ref[...], u_ref[...])
    ar_ref[...] = res[:n1].astype(BF16)
    ai_ref[...] = res[n1:].astype(BF16)


def _merged(groups, g):
    return lambda rows, c: pl.BlockSpec((None, rows, c), lambda b, j: (b, 0, j * groups + g))


def _split(g):
    return lambda rows, c: pl.BlockSpec((None, None, rows, c), lambda b, j: (b, j, 0, g))


def _fft_a(src, layout, c, n2, f_a):
    bn = src.shape[0]
    n1 = f_a.shape[0] // 2
    out = jax.ShapeDtypeStruct((bn, n1, n2 * c), BF16)
    ospec = pl.BlockSpec((None, n1, c), lambda b, j: (b, 0, j))
    ar, ai = pl.pallas_call(
        _fft_a_kernel,
        grid=(bn, n2),
        in_specs=[layout(n1 // 2, c), pl.BlockSpec(f_a.shape, lambda b, j: (0, 0))],
        out_specs=[ospec, ospec],
        out_shape=[out, out],
        compiler_params=_params("parallel", "parallel"),
        name="fft_a",
    )(src, f_a)
    return ar.reshape(bn, n1, n2, c), ai.reshape(bn, n1, n2, c)


def _stack(r_ref, i_ref):
    return jnp.concatenate([r_ref[...], i_ref[...]], axis=0)


def _fft_spec_kernel(fr_ref, fi_ref, br_ref, bi_ref, g_ref, kr_ref, ki_ref):
    n2 = kr_ref.shape[0]
    xf = jnp.dot(g_ref[...], _stack(fr_ref, fi_ref), preferred_element_type=F32)
    xb = jnp.dot(g_ref[...], _stack(br_ref, bi_ref), preferred_element_type=F32)
    kr_ref[...] = xf[:n2] + xb[:n2]
    ki_ref[...] = xf[n2:] - xb[n2:]


def _fft_spec(tr, ti, g):
    _, n1, n2, c = tr.shape
    fwd = pl.BlockSpec((None, None, n2, c), lambda o, k: (o, k, 0, 0))
    bwd = pl.BlockSpec((None, None, n2, c), lambda o, k: (2 + o, k, 0, 0))
    out = jax.ShapeDtypeStruct((2, n1, n2, c), F32)
    return pl.pallas_call(
        _fft_spec_kernel,
        grid=(2, n1),
        in_specs=[fwd, fwd, bwd, bwd, pl.BlockSpec((None, 2 * n2, 2 * n2), lambda o, k: (k, 0, 0))],
        out_specs=[fwd, fwd],
        out_shape=[out, out],
        compiler_params=_params("parallel", "parallel"),
        name="fft_spec",
    )(tr, ti, tr, ti, g)


def _fft_mid_kernel(ar_ref, ai_ref, g_ref, gh_ref, kr_ref, ki_ref, zr_ref, zi_ref):
    n2 = zr_ref.shape[0]
    x = jnp.dot(g_ref[...], _stack(ar_ref, ai_ref), preferred_element_type=F32)
    xr, xi = x[:n2], x[n2:]
    kr, ki = kr_ref[...], ki_ref[...]
    y = jnp.concatenate([(xr * kr - xi * ki).astype(BF16), (xr * ki + xi * kr).astype(BF16)], axis=0)
    z = jnp.dot(gh_ref[...], y, preferred_element_type=F32)
    zr_ref[...] = z[:n2].astype(BF16)
    zi_ref[...] = z[n2:].astype(BF16)


def _fft_mid(ar, ai, g, gh, kr, ki, order):
    bn, n1, n2, c = ar.shape
    act = pl.BlockSpec((None, None, n2, c), lambda k, b: (b, k, 0, 0))
    mat = pl.BlockSpec((None, 2 * n2, 2 * n2), lambda k, b: (k, 0, 0))
    spec = pl.BlockSpec((None, None, n2, c), lambda k, b: (order, k, 0, 0))
    out = jax.ShapeDtypeStruct((bn, n1, n2, c), BF16)
    return pl.pallas_call(
        _fft_mid_kernel,
        grid=(n1, bn),
        in_specs=[act, act, mat, mat, spec, spec],
        out_specs=[act, act],
        out_shape=[out, out],
        compiler_params=_params("parallel", "parallel"),
        name="fft_mid",
    )(ar, ai, g, gh, kr, ki)


def _fft_c_kernel(zr_ref, zi_ref, f_ref, u_ref, gate_ref, bias_ref, o_ref):
    conv = jnp.dot(f_ref[...], _stack(zr_ref, zi_ref), preferred_element_type=F32)
    u = u_ref[...]
    o_ref[...] = gate_ref[...] * (conv + bias_ref[...] * u)


def _fft_c(zr, zi, f_c, u, u_layout, gate, gate_layout, bias, order):
    bn, n1, n2, c = zr.shape
    zspec = pl.BlockSpec((None, n1, c), lambda b, j: (b, 0, j))
    return pl.pallas_call(
        _fft_c_kernel,
        grid=(bn, n2),
        in_specs=[zspec, zspec, pl.BlockSpec(f_c.shape, lambda b, j: (0, 0)), u_layout(n1 // 2, c),
                  gate_layout(n1 // 2, c), pl.BlockSpec((None, 1, c), lambda b, j: (order, 0, 0))],
        out_specs=_merged(1, 0)(n1 // 2, c),
        out_shape=jax.ShapeDtypeStruct((bn, n1 // 2, n2 * c), F32),
        compiler_params=_params("parallel", "parallel"),
        name="fft_c",
    )(zr.reshape(bn, n1, n2 * c), zi.reshape(bn, n1, n2 * c), f_c, u, gate, bias)


def _hy_ctx_kernel(x1_ref, x2_ref, v_ref, taps_ref, bias_ref, fwd_ref, inv_ref, o_ref):
    n = fwd_ref.shape[0] // 2
    fwd, inv = fwd_ref[...], inv_ref[...]

    def spec(a):
        s = _dot_f32(fwd, a)
        return s[:n], s[n:]

    u = v_ref[...]
    for o, gate_ref in ((0, x1_ref), (1, x2_ref)):
        fr, fi = spec(taps_ref[o])
        br, bi = spec(taps_ref[2 + o])
        kr, ki = fr + br, fi - bi
        ur, ui = spec(u)
        y = jnp.concatenate([ur * kr - ui * ki, ur * ki + ui * kr], axis=0)
        u = gate_ref[...] * (_dot_f32(inv, y) + bias_ref[o:o + 1, :] * u)
    o_ref[...] = u


def _hy_ctx(zc, taps, bias, seq, ctx_len, c):
    b = zc.shape[0]
    n = 2 * ctx_len
    tc = _pick(c, (256, 128))
    nct = c // tc
    k = jnp.arange(n, dtype=jnp.int32)[:, None]
    m = jnp.arange(ctx_len, dtype=jnp.int32)[None, :]
    ang = ((k * m) % n).astype(F32) * (2 * math.pi / n)
    fwd = jnp.concatenate([jnp.cos(ang), -jnp.sin(ang)], axis=0)
    inv = jnp.concatenate([jnp.cos(ang).T, -jnp.sin(ang).T], axis=1) / n
    row = seq // ctx_len
    tok = lambda g: pl.BlockSpec((None, ctx_len, tc), lambda j, i: (i, row, g * nct + j))
    return pl.pallas_call(
        _hy_ctx_kernel,
        grid=(nct, b),
        in_specs=[tok(0), tok(1), tok(2),
                  pl.BlockSpec((4, ctx_len, tc), lambda j, i: (0, 0, j)),
                  pl.BlockSpec((2, tc), lambda j, i: (0, j)),
                  pl.BlockSpec(fwd.shape, lambda j, i: (0, 0)),
                  pl.BlockSpec(inv.shape, lambda j, i: (0, 0))],
        out_specs=pl.BlockSpec((None, ctx_len, tc), lambda j, i: (i, 0, j)),
        out_shape=jax.ShapeDtypeStruct((b, ctx_len, c), F32),
        compiler_params=_params("parallel", "parallel"),
        name="hy_ctx",
    )(zc, zc, zc, taps, bias, fwd, inv)


def _hyena(zh, hy, seq, ctx_len, tables):
    short_w, short_b, w1, b1, w2, b2, freq, w3, bias = hy
    b, t, c3 = zh.shape
    c = c3 // 3
    f_a, f_c, g, gh = tables
    n2 = LANES
    zc = _time_mix(_short_conv_kernel, zh, jnp.concatenate([short_w, short_b[None]], axis=0), seq, "short_conv")

    hf = w2.shape[0]
    hp = LANES
    pad_c = lambda a: jnp.pad(a, ((0, 0), (0, hp - hf)))
    bands = jnp.pad(jnp.linspace(1e-4, HY_BANDS - 1, HY_BANDS, dtype=F32)[None, :], ((0, 0), (0, LANES - HY_BANDS)))
    w1c = jnp.pad(w1[1:1 + HY_BANDS], ((0, LANES - HY_BANDS), (0, hp - hf)))
    w1s = jnp.pad(w1[1 + HY_BANDS:], ((0, LANES - HY_BANDS), (0, hp - hf)))
    hid_args = (bands, pad_c(w1[0:1]), w1c, w1s, pad_c(b1[None]), jnp.pad(w2, ((0, hp - hf), (0, hp - hf))),
                pad_c(b2[None]), pad_c(freq))
    w3p = jnp.pad(w3, ((0, hp - hf), (0, 0)))
    max_decay = math.log(HY_TARGET) / HY_FAST_PCT
    min_decay = math.log(HY_TARGET) / HY_SLOW_PCT
    delta = jnp.abs(jnp.linspace(min_decay, max_decay, c, dtype=F32))[None, :]

    taps_l = _hy_taps(_hy_hidden(seq, *hid_args), w3p, delta, c)
    taps_c = _hy_taps(_hy_hidden(ctx_len, *hid_args), w3p, delta, c)

    as_split = lambda a: jnp.swapaxes(a.reshape(a.shape[0], seq // n2, n2, a.shape[-1]), 1, 2)
    tr, ti = _fft_a(as_split(taps_l), _split(0), c, n2, f_a)
    kr, ki = _fft_spec(tr, ti, g)

    zct = as_split(zc[:, :seq])
    bias3 = bias.reshape(2, 1, c)
    ar, ai = _fft_a(zct, _split(2), c, n2, f_a)
    zr, zi = _fft_mid(ar, ai, g, gh, kr, ki, 0)
    u1 = _fft_c(zr, zi, f_c, zct, _split(2), zct, _split(0), bias3, 0)
    ar, ai = _fft_a(u1, _merged(1, 0), c, n2, f_a)
    zr, zi = _fft_mid(ar, ai, g, gh, kr, ki, 1)
    u2 = _fft_c(zr, zi, f_c, u1, _merged(1, 0), zct, _split(1), bias3, 1)
    out_c = _hy_ctx(zc, taps_c, bias, seq, ctx_len, c)
    return jnp.concatenate([u2.reshape(b, seq, c), out_c], axis=1)


def _head_sum(x, n_heads):
    fold = x[:, :LANES]
    for s in range(LANES, x.shape[1], LANES):
        fold = fold + x[:, s:s + LANES]
    r = lax.broadcasted_iota(jnp.int32, (LANES, LANES), 0) & (n_heads - 1)
    c = lax.broadcasted_iota(jnp.int32, (LANES, LANES), 1) & (n_heads - 1)
    per_head = _dot_f32(fold, jnp.where(r == c, 1.0, 0.0))
    return jnp.concatenate([per_head] * (x.shape[1] // LANES), axis=1)


def _rw_feat_kernel(r_ref, k_ref, v_ref, zl_ref, wup_ref, aup_ref, gup_ref, w0_ref, a0_ref, kkp_ref, kap_ref,
                    rk_ref, kk_o, g_o, bonus_o, w_o, kd_o, kka_o, *, n_heads):
    r, k, v = r_ref[...], k_ref[...], v_ref[...]
    zl = zl_ref[...]
    wl, al, gl = zl[:, :LANES], zl[:, LANES:2 * LANES], zl[:, 2 * LANES:]
    kq = k * kkp_ref[...]
    kk = kq * lax.rsqrt(_head_sum(kq * kq, n_heads) + 1e-12)
    kk_o[...] = kk
    g_o[...] = _dot(_sigmoid(gl), gup_ref[...])
    twl = jnp.tanh(wl)
    bonus = jnp.zeros_like(v)
    for d in range(2):
        w = jnp.exp(-RW_DECAY_SCALE * _sigmoid(w0_ref[d:d + 1, :] + _dot_f32(twl, wup_ref[d])))
        a = _sigmoid(a0_ref[d:d + 1, :] + _dot_f32(al, aup_ref[d]))
        kd = k * (1.0 + (a - 1.0) * kap_ref[...])
        w_o[d] = w
        kd_o[d] = kd
        kka_o[d] = kk * a
        bonus = bonus + _head_sum(r * kd * rk_ref[...], n_heads) * v
    bonus_o[...] = bonus


def _rw_features(zs, zl, w_up, a_up, g_up, w0, a0, k_k, k_a, r_k, tr):
    b, t, c3 = zs.shape
    c = c3 // 3
    tok = lambda g: pl.BlockSpec((None, tr, c), lambda i, j: (i, j, g))
    full = lambda a: pl.BlockSpec(a.shape, lambda i, j: (0,) * a.ndim)
    two = pl.BlockSpec((2, None, tr, c), lambda i, j: (0, i, j, 0))
    s1 = jax.ShapeDtypeStruct((b, t, c), F32)
    s2 = jax.ShapeDtypeStruct((2, b, t, c), F32)
    consts = (w_up, a_up, g_up, w0, a0, k_k, k_a, r_k)
    return pl.pallas_call(
        functools.partial(_rw_feat_kernel, n_heads=c // RW_HEAD),
        grid=(b, t // tr),
        in_specs=[tok(0), tok(1), tok(2), pl.BlockSpec((None, tr, zl.shape[2]), lambda i, j: (i, j, 0))]
        + [full(a) for a in consts],
        out_specs=[tok(0), tok(0), tok(0), two, two, two],
        out_shape=[s1, s1, s1, s2, s2, s2],
        compiler_params=_params("parallel", "parallel"),
        name="rw_features",
    )(zs, zs, zs, zl, *consts)


RW_STEPS = 32
RW_PARTIALS = 4
RW_UNROLL = 2


def _replicated_row(ref, o, b, r):
    return ref[o, b, pl.ds(r, SUBLANES, stride=0), :]


def _rw_scan_kernel(*refs, reverse, n_heads, accumulate):
    if accumulate:
        r_ref, w_ref, kd_ref, v_ref, kk_ref, kka_ref, yin_ref, y_ref, s_ref, rows_ref = refs
    else:
        r_ref, w_ref, kd_ref, v_ref, kk_ref, kka_ref, y_ref, s_ref, rows_ref = refs
    n_batch, n_tiles = s_ref.shape[:2]
    steps = v_ref.shape[1]

    @pl.when(pl.program_id(0) == 0)
    def _():
        s_ref[...] = jnp.zeros_like(s_ref)

    operands = (kk_ref, kka_ref, kd_ref, w_ref, r_ref)
    for o, ref in enumerate(operands):
        for b in range(n_batch):
            for j in range(n_tiles):
                rows_ref[o, b, pl.ds(j, steps, stride=n_tiles), :] = ref[b, :, j * LANES:(j + 1) * LANES]
    KK, KKA, KD, W, R = range(len(operands))

    def total(parts):
        while len(parts) > 1:
            parts = [a + b for a, b in zip(parts[::2], parts[1::2])]
        x = parts[0].reshape(RW_HEAD, LANES)
        shift = LANES // 2
        while shift >= n_heads:
            x = x + pltpu.roll(x, shift, 1)
            shift //= 2
        return x.reshape(parts[0].shape)

    def accumulate_into(parts, j, term):
        parts[j % RW_PARTIALS] = term if parts[j % RW_PARTIALS] is None else parts[j % RW_PARTIALS] + term

    def one_step(b, t):
        row = lambda o, j: _replicated_row(rows_ref, o, b, t * n_tiles + j)[None]
        parts = [None] * RW_PARTIALS
        for j in range(n_tiles):
            accumulate_into(parts, j, s_ref[b, j] * row(KK, j))
        removed = total(parts)
        v = v_ref[b, t].reshape(removed.shape)
        parts = [None] * RW_PARTIALS
        for j in range(n_tiles):
            s = s_ref[b, j] * row(W, j) - removed * row(KKA, j) + v * row(KD, j)
            s_ref[b, j] = s
            accumulate_into(parts, j, s * row(R, j))
        y = total(parts).reshape(v_ref.shape[2:])
        y_ref[b, t] = (yin_ref[b, t] + y) if accumulate else y

    def group(gi, carry):
        for n in range(RW_UNROLL):
            i = gi * RW_UNROLL + n
            t = (steps - 1 - i) if reverse else i
            for b in range(n_batch):
                one_step(b, t)
        return carry

    lax.fori_loop(0, steps // RW_UNROLL, group, 0)


def _segment_order(n_lat, n_ctx, reverse):
    if reverse:
        return lambda j: n_lat + n_ctx - 1 - j
    return lambda j: jnp.where(j < n_ctx, n_lat + j, j - n_ctx)


def _rw_scan(zs, w, kd, vt, kk, kka, seq, direction, y_prev):
    b, t, c = kk.shape
    reverse = direction == 1
    order = _segment_order(seq // RW_STEPS, (t - seq) // RW_STEPS, reverse)
    tok = pl.BlockSpec((b, RW_STEPS, c), lambda j: (0, order(j), 0))
    dtok = pl.BlockSpec((None, b, RW_STEPS, c), lambda j: (direction, 0, order(j), 0))
    val = pl.BlockSpec((b, RW_STEPS, RW_HEAD, LANES), lambda j: (0, order(j), 0, 0))
    accumulate = y_prev is not None
    args = (zs, w, kd, vt, kk, kka) + ((y_prev,) if accumulate else ())
    n_tiles = c // LANES
    return pl.pallas_call(
        functools.partial(_rw_scan_kernel, reverse=reverse, n_heads=c // RW_HEAD, accumulate=accumulate),
        grid=(t // RW_STEPS,),
        in_specs=[tok, dtok, dtok, val, tok, dtok] + ([val] if accumulate else []),
        out_specs=val,
        out_shape=jax.ShapeDtypeStruct(vt.shape, F32),
        scratch_shapes=[pltpu.VMEM((b, n_tiles, RW_HEAD // SUBLANES, SUBLANES, LANES), F32),
                        pltpu.VMEM((5, b, RW_STEPS * n_tiles, LANES), F32)],
        input_output_aliases={6: 0} if accumulate else {},
        compiler_params=_params("arbitrary"),
        name="rw_scan",
    )(*args)


def _rw_out_kernel(y_ref, bonus_ref, g_ref, lnw_ref, lnb_ref, o_ref, *, n_heads):
    y = y_ref[...]
    mean = _head_sum(y, n_heads) * (1.0 / RW_HEAD)
    yc = y - mean
    var = _head_sum(yc * yc, n_heads) * (1.0 / RW_HEAD)
    yn = yc * lax.rsqrt(var + RW_GN_EPS) * lnw_ref[...] + lnb_ref[...]
    o_ref[...] = ((yn + bonus_ref[...]) * g_ref[...]).astype(o_ref.dtype)


def _rw_out(y, bonus, g, ln_w, ln_b, tr):
    b, t, c = y.shape
    tok = pl.BlockSpec((None, tr, c), lambda i, j: (i, j, 0))
    vec = pl.BlockSpec((1, c), lambda i, j: (0, 0))
    return pl.pallas_call(
        functools.partial(_rw_out_kernel, n_heads=c // RW_HEAD),
        grid=(b, t // tr),
        in_specs=[tok, tok, tok, vec, vec],
        out_specs=tok,
        out_shape=jax.ShapeDtypeStruct((b, t, c), BF16),
        compiler_params=_params("parallel", "parallel"),
        name="rw_out",
    )(y, bonus, g, ln_w, ln_b)


def _pad_rows(a, rows):
    return jnp.pad(a, [(0, 0)] * (a.ndim - 2) + [(0, rows - a.shape[-2]), (0, 0)])


def _rw_perm(c):
    return jnp.arange(c).reshape(c // RW_HEAD, RW_HEAD).T.reshape(-1)


def _rwkv(zs_raw, zl_raw, rw, seq, tr):
    mu, w0, w_up, a0, a_up, g_up, k_k, k_a, r_k, ln_w, ln_b = rw
    b, t, _ = zs_raw.shape
    c = k_k.shape[0]
    n_heads = c // RW_HEAD
    perm = _rw_perm(c)
    pc = lambda a: jnp.take(a, perm, axis=-1)
    lw, la = w_up.shape[1], a_up.shape[1]
    mu_l = mu[3 * c:]
    pad_to = lambda a: jnp.pad(a, (0, LANES - a.shape[0]))
    mu_lora = jnp.concatenate([pad_to(mu_l[:lw]), pad_to(mu_l[lw:lw + la]), mu_l[lw + la:]])
    mu_main = pc(mu[:3 * c].reshape(3, c)).reshape(1, 3 * c)
    zs = _time_mix(_token_shift_kernel, zs_raw, mu_main, seq, "token_shift")
    zl = _time_mix(_token_shift_kernel, zl_raw, mu_lora[None, :], seq, "token_shift")
    kk, g, bonus, w, kd, kka = _rw_features(
        zs, zl, _pad_rows(pc(w_up), LANES), _pad_rows(pc(a_up), LANES), pc(g_up).astype(BF16), pc(w0), pc(a0),
        pc(k_k)[None], pc(k_a)[None], pc(r_k.reshape(c))[None], min(tr, 128))

    vt = jnp.tile(zs[..., 2 * c:].reshape(b, t, RW_HEAD, n_heads), (1, 1, 1, LANES // n_heads))
    y = _rw_scan(zs, w, kd, vt, kk, kka, seq, 0, None)
    y = _rw_scan(zs, w, kd, vt, kk, kka, seq, 1, y)
    y = y[..., :n_heads].reshape(b, t, c)
    return _rw_out(y, bonus, g, pc(ln_w)[None], pc(ln_b)[None], tr)


def _hg_bounds_kernel(x_ref, o_ref):
    x = x_ref[...]
    e = jnp.exp(x - jnp.max(x, axis=0, keepdims=True))
    p = e / jnp.sum(e, axis=0, keepdims=True)
    run = jnp.zeros_like(p[0:1])
    o_ref[0:1, :] = run
    for i in range(1, x.shape[0]):
        run = run + p[i:i + 1]
        o_ref[i:i + 1, :] = run


def _hg_bounds(lower):
    return pl.pallas_call(
        _hg_bounds_kernel,
        out_shape=jax.ShapeDtypeStruct(lower.shape, F32),
        name="hg_bounds",
    )(lower)


HG_HEADS_PER_STEP = 16


def _hg_scan_kernel(q_ref, f_ref, i_ref, lb_ref, o_ref, s_ref, *, reverse):
    cs = q_ref.shape[0]
    n_heads = s_ref.shape[0]

    @pl.when(pl.program_id(2) == 0)
    def _():
        s_ref[...] = jnp.zeros_like(s_ref)

    width = q_ref.shape[1]
    row = lax.broadcasted_iota(jnp.int32, (cs, cs), 0)
    col = lax.broadcasted_iota(jnp.int32, (cs, cs), 1)
    row_w = lax.broadcasted_iota(jnp.int32, (cs, width), 0)
    inclusive = lambda upto: jnp.where((col >= upto) if reverse else (col <= upto), 1.0, 0.0)
    sums = [inclusive(row)]
    levels = []
    hs = cs // 2
    while hs >= 1:
        blk = 2 * hs
        boundary = (row & ~(blk - 1)) + hs
        sums.append(inclusive(boundary if reverse else boundary - 1))
        levels.append(((row & ~(blk - 1)) == (col & ~(blk - 1)), (row_w & (blk - 1)) >= hs))
        hs //= 2

    heads = [slice(h * HG_HEAD, (h + 1) * HG_HEAD) for h in range(n_heads)]
    q, v, lb = q_ref[...], i_ref[...], lb_ref[...]
    f = lb + (1.0 - lb) * _sigmoid(f_ref[...])
    k = 1.0 - f
    bb = _dot_f32(jnp.concatenate(sums, axis=0), jnp.log(f))
    b = bb[:cs]
    qk = q * k
    att = [jnp.where(row == col, jnp.sum(qk[:, sl], axis=-1, keepdims=True), 0.0) for sl in heads]
    for l, (same, late) in enumerate(levels):
        ref_b = bb[(l + 1) * cs:(l + 2) * cs]
        q_side, k_side = (jnp.logical_not(late), late) if reverse else (late, jnp.logical_not(late))
        qt = (q * jnp.exp(jnp.where(q_side, b - ref_b, -jnp.inf))).astype(BF16)
        kt = (k * jnp.exp(jnp.where(k_side, ref_b - b, -jnp.inf))).astype(BF16)
        for h, sl in enumerate(heads):
            att[h] = att[h] + jnp.where(same, _dot_nt(qt[:, sl], kt[:, sl]), 0.0)
    b_end = b[0:1, :] if reverse else b[cs - 1:cs, :]
    qe = (q * jnp.exp(b)).astype(BF16)
    ke = (k * jnp.exp(b_end - b)).astype(BF16)
    decay = jnp.exp(b_end)
    vb = v.astype(BF16)
    for h, sl in enumerate(heads):
        st = s_ref[h]
        o_ref[:, sl] = _dot(att[h], vb[:, sl]) + _dot_nt(qe[:, sl], st)
        s_ref[h] = decay[:, sl] * st + _dot_tn(vb[:, sl], ke[:, sl])


def _hg_scan(p, lb, c, seq, direction):
    b, t, _ = p.shape
    wl = min(HG_HEADS_PER_STEP * HG_HEAD, c)
    ng = c // wl
    reverse = direction == 1
    order = _segment_order(seq // HG_CHUNK, (t - seq) // HG_CHUNK, reverse)
    blk = lambda g: pl.BlockSpec((None, HG_CHUNK, wl), lambda i, n, j: (i, order(j), g * ng + n))
    return pl.pallas_call(
        functools.partial(_hg_scan_kernel, reverse=reverse),
        grid=(b, ng, t // HG_CHUNK),
        in_specs=[blk(0), blk(1 + direction), blk(3), pl.BlockSpec((1, wl), lambda i, n, j: (0, n))],
        out_specs=blk(0),
        out_shape=jax.ShapeDtypeStruct((b, t, c), F32),
        scratch_shapes=[pltpu.VMEM((wl // HG_HEAD, HG_HEAD, HG_HEAD), F32)],
        compiler_params=_params("parallel", "parallel", "arbitrary"),
        name="hg_scan",
    )(p, p, p, lb)


def _hg_out_kernel(o0_ref, o1_ref, g_ref, w_ref, out_ref):
    o = o0_ref[...] + o1_ref[...]
    g = g_ref[...]
    for s in range(0, o.shape[1], HG_HEAD):
        oh = o[:, s:s + HG_HEAD]
        gh = g[:, s:s + HG_HEAD]
        out_ref[:, s:s + HG_HEAD] = (_rms(oh) * w_ref[...] * (gh * _sigmoid(gh))).astype(out_ref.dtype)


def _hg_out(o0, o1, p, norm_w, tr):
    b, t, c = o0.shape
    tc = _pick(c, (512, 256, 128))
    nct = c // tc
    tok = pl.BlockSpec((None, tr, tc), lambda i, j, n: (i, j, n))
    return pl.pallas_call(
        _hg_out_kernel,
        grid=(b, t // tr, nct),
        in_specs=[tok, tok, pl.BlockSpec((None, tr, tc), lambda i, j, n: (i, j, 4 * nct + n)),
                  pl.BlockSpec((1, HG_HEAD), lambda i, j, n: (0, 0))],
        out_specs=tok,
        out_shape=jax.ShapeDtypeStruct((b, t, c), BF16),
        compiler_params=_params("parallel", "parallel", "parallel"),
        name="hg_out",
    )(o0, o1, p, norm_w)


def _rope_tables(seq):
    t = jnp.arange(seq)
    axis = AT_HEAD // 2
    inv = ROPE_THETA ** (-jnp.arange(0, axis, 2, dtype=F32) / axis)
    ang = jnp.concatenate([(t // GRID_W)[:, None] * inv, (t % GRID_W)[:, None] * inv], axis=-1)
    sign = jnp.tile(jnp.array([-1.0, 1.0], F32), AT_HEAD // 2)
    return jnp.repeat(jnp.cos(ang), 2, axis=-1), jnp.repeat(jnp.sin(ang), 2, axis=-1) * sign


def _at_prep_kernel(q_ref, k_ref, qw_ref, kw_ref, cos_ref, sin_ref, qo_ref, ko_ref, *, n_lat):
    is_lat = pl.program_id(1) < n_lat
    cos, sin = cos_ref[...], sin_ref[...]
    even = (lax.broadcasted_iota(jnp.int32, cos.shape, 1) & 1) == 0

    def head(x, w):
        y = _rms(x) * w
        partner = jnp.where(even, pltpu.roll(y, AT_HEAD - 1, 1), pltpu.roll(y, 1, 1))
        return jnp.where(is_lat, y * cos + partner * sin, y)

    scale = AT_HEAD ** -0.5
    for s in range(0, qo_ref.shape[1], AT_HEAD):
        qo_ref[:, s:s + AT_HEAD] = (head(q_ref[:, s:s + AT_HEAD], qw_ref[...]) * scale).astype(qo_ref.dtype)
    for s in range(0, ko_ref.shape[1], AT_HEAD):
        ko_ref[:, s:s + AT_HEAD] = head(k_ref[:, s:s + AT_HEAD], kw_ref[...]).astype(ko_ref.dtype)


def _at_prep(p, c, q_norm, k_norm, cos, sin, seq, tr):
    b, t, _ = p.shape
    kvw = AT_KV_HEADS * AT_HEAD
    n_lat = seq // tr
    tab = pl.BlockSpec((tr, AT_HEAD), lambda i, j: (jnp.minimum(j, n_lat - 1), 0))
    vec = pl.BlockSpec((1, AT_HEAD), lambda i, j: (0, 0))
    return pl.pallas_call(
        functools.partial(_at_prep_kernel, n_lat=n_lat),
        grid=(b, t // tr),
        in_specs=[pl.BlockSpec((None, tr, c), lambda i, j: (i, j, 5)),
                  pl.BlockSpec((None, tr, kvw), lambda i, j: (i, j, 6 * c // kvw)),
                  vec, vec, tab, tab],
        out_specs=[pl.BlockSpec((None, tr, c), lambda i, j: (i, j, 0)),
                   pl.BlockSpec((None, tr, kvw), lambda i, j: (i, j, 0))],
        out_shape=[jax.ShapeDtypeStruct((b, t, c), BF16), jax.ShapeDtypeStruct((b, t, kvw), BF16)],
        compiler_params=_params("parallel", "parallel"),
        name="at_prep",
    )(p, p, q_norm, k_norm, cos, sin)


AT_Q_BLOCK = 128
AT_AHEAD = 3


def _flash_kernel(q_ref, k_ref, v_ref, o_ref, m_ref, l_ref, acc_ref):
    kv = pl.program_id(3)

    @pl.when(kv == 0)
    def _():
        m_ref[...] = jnp.full_like(m_ref, -jnp.inf)
        l_ref[...] = jnp.zeros_like(l_ref)
        acc_ref[...] = jnp.zeros_like(acc_ref)

    k = k_ref[...]
    v = v_ref[...].astype(BF16)
    group, tq, _ = m_ref.shape
    qb = min(AT_Q_BLOCK, tq)
    blocks = [(r, i) for r in range(group) for i in range(tq // qb)]

    def scores(r, i):
        return _dot_nt(q_ref[i * qb:(i + 1) * qb, r * AT_HEAD:(r + 1) * AT_HEAD], k)

    ahead = [scores(*blk) for blk in blocks[:AT_AHEAD]]
    for n, (r, i) in enumerate(blocks):
        s = ahead.pop(0)
        if n + AT_AHEAD < len(blocks):
            ahead.append(scores(*blocks[n + AT_AHEAD]))
        rows = slice(i * qb, (i + 1) * qb)
        m_prev = m_ref[r, rows, :]
        m_cur = jnp.maximum(m_prev, jnp.max(s, axis=-1, keepdims=True))
        alpha = jnp.exp(m_prev - m_cur)
        p = jnp.exp(s - m_cur[:, 0:1])
        l_ref[r, rows, :] = alpha * l_ref[r, rows, :] + jnp.sum(p, axis=-1, keepdims=True)
        acc_ref[r, rows, :] = alpha * acc_ref[r, rows, :] + _dot(p, v)
        m_ref[r, rows, :] = m_cur

    @pl.when(kv == pl.num_programs(3) - 1)
    def _():
        for r in range(group):
            o_ref[:, r * AT_HEAD:(r + 1) * AT_HEAD] = (acc_ref[r] / l_ref[r]).astype(o_ref.dtype)


def _flash(qn, kn, p, c, q_blk0, tq, nq, kv_blk0, tk, nk):
    b = qn.shape[0]
    group = c // AT_HEAD // AT_KV_HEADS
    gw = group * AT_HEAD
    v_col0 = (6 * c + AT_KV_HEADS * AT_HEAD) // AT_HEAD
    return pl.pallas_call(
        _flash_kernel,
        grid=(b, AT_KV_HEADS, nq, nk),
        in_specs=[pl.BlockSpec((None, tq, gw), lambda i, g, a, n: (i, q_blk0 + a, g)),
                  pl.BlockSpec((None, tk, AT_HEAD), lambda i, g, a, n: (i, kv_blk0 + n, g)),
                  pl.BlockSpec((None, tk, AT_HEAD), lambda i, g, a, n: (i, kv_blk0 + n, v_col0 + g))],
        out_specs=pl.BlockSpec((None, tq, gw), lambda i, g, a, n: (i, a, g)),
        out_shape=jax.ShapeDtypeStruct((b, nq * tq, c), BF16),
        scratch_shapes=[pltpu.VMEM((group, tq, AT_HEAD), F32)] * 3,
        compiler_params=_params("parallel", "parallel", "parallel", "arbitrary"),
        name="flash",
    )(qn, kn, p)


def kernel(x, c, ctx, c_ctx, ada_down, ada_up, ada_bias, norm_g, ffn_w1, ffn_w3, ffn_w2, ev_w_in, ev_w_out, hy_short_w, hy_short_b, hy_pe_w1, hy_pe_b1, hy_pe_w2, hy_pe_b2, hy_sin_freq, hy_pe_w3, hy_bias, rw_mu, rw_w0, rw_w_up, rw_a0, rw_a_up, rw_g_up, rw_k_k, rw_k_a, rw_r_k, rw_ln_w, rw_ln_b, od_w_in, od_w_out, hg_lower_bounds, hg_norm_g, at_q_norm, at_k_norm):
    bsz, seq, d = x.shape
    ctx_len = ctx.shape[1]
    t = seq + ctx_len
    depth = ada_down.shape[0]
    half = d // 2
    tr = min(256, ctx_len)
    assert seq % ctx_len == 0 and ctx_len % LANES == 0 and bsz < SUBLANES

    xs = jnp.concatenate([x, ctx], axis=1)
    cond = jnp.concatenate([c, c_ctx[None], jnp.zeros((SUBLANES - bsz - 1, d), F32)], axis=0)
    m = _adaln(cond, ada_down, ada_up, ada_bias).reshape(depth, SUBLANES, N_MOD, d)
    mods = jnp.stack([m[:, :bsz], jnp.broadcast_to(m[:, bsz:bsz + 1], (depth, bsz, N_MOD, d))], axis=2)

    n1 = 2 * seq // LANES
    tables = _fft_tables(n1, LANES)
    cos, sin = _rope_tables(seq)
    lb_all = _hg_bounds(hg_lower_bounds)

    for l in range(depth):
        h = _norm_mod(xs, norm_g, mods, l, seq, tr, 0, 0, 1).reshape(bsz * t, d)
        if l % 2 == 0:
            e = l // 2
            w_in = ev_w_in[e]
            n_hy = 3 * half
            lw, la = rw_w_up.shape[2], rw_a_up.shape[2]
            w_lo = w_in[:, 2 * n_hy:]
            pad_c = lambda a: jnp.pad(a, ((0, 0), (0, LANES - a.shape[1])))
            w_lora = jnp.concatenate([pad_c(w_lo[:, :lw]), pad_c(w_lo[:, lw:lw + la]), w_lo[:, lw + la:]], axis=1)
            n_rw_heads = half // RW_HEAD
            w_rkv = w_in[:, n_hy:2 * n_hy].reshape(d, 3, n_rw_heads, RW_HEAD).swapaxes(2, 3).reshape(d, n_hy)
            w_main = jnp.concatenate([w_in[:, :n_hy], w_rkv], axis=1).astype(BF16)
            zh = _mm([h], w_main, 0, n_hy, F32).reshape(bsz, t, n_hy)
            zs = _mm([h], w_main, n_hy, n_hy, F32).reshape(bsz, t, n_hy)
            zl = _mm([h], w_lora.astype(BF16), 0, w_lora.shape[1], F32).reshape(bsz, t, -1)
            hy = (hy_short_w[e], hy_short_b[e], hy_pe_w1[e], hy_pe_b1[e], hy_pe_w2[e], hy_pe_b2[e],
                  hy_sin_freq[e], hy_pe_w3[e], hy_bias[e])
            rw = (rw_mu[e], rw_w0[e], rw_w_up[e], rw_a0[e], rw_a_up[e], rw_g_up[e], rw_k_k[e],
                  rw_k_a[e], rw_r_k[e], rw_ln_w[e], rw_ln_b[e])
            mix_a = _hyena(zh, hy, seq, ctx_len, tables).astype(BF16)
            mix_b = _rwkv(zs, zl, rw, seq, tr)
            w_rw_rows = ev_w_out[e][half:].reshape(n_rw_heads, RW_HEAD, d).swapaxes(0, 1).reshape(half, d)
            w_out = jnp.concatenate([ev_w_out[e][:half], w_rw_rows], axis=0)
        else:
            o = l // 2
            p = _mm([h], od_w_in[o].astype(BF16), 0, od_w_in.shape[2], F32).reshape(bsz, t, -1)
            lb = lb_all[l][None, :]
            o0 = _hg_scan(p, lb, half, seq, 0)
            o1 = _hg_scan(p, lb, half, seq, 1)
            mix_a = _hg_out(o0, o1, p, hg_norm_g[o][None], tr)
            qn, kn = _at_prep(p, half, at_q_norm[o][None], at_k_norm[o][None], cos, sin, seq, tr)
            tq = _pick(seq, (1024, 512, 256, 128))
            tk = _pick(t, (2816, 768, 512, 256, 128))
            at_l = _flash(qn, kn, p, half, 0, tq, seq // tq, 0, tk, t // tk)
            at_c = _flash(qn, kn, p, half, seq // ctx_len, ctx_len, 1, seq // ctx_len, ctx_len, 1)
            mix_b = jnp.concatenate([at_l, at_c], axis=1)
            w_out = od_w_out[o]
        y = _mm([mix_a.reshape(bsz * t, half), mix_b.reshape(bsz * t, half)], w_out.astype(BF16), 0, d, F32)
        xs = _resid(xs, y, norm_g, mods, l, seq, tr, 1, 2)
        h = _norm_mod(xs, norm_g, mods, l, seq, tr, 2, 3, 4).reshape(bsz * t, d)
        hid = _swiglu_up(h, ffn_w1[l].astype(BF16), ffn_w3[l].astype(BF16))
        y = _mm_ksplit(hid, ffn_w2[l].astype(BF16))
        xs = _resid(xs, y, norm_g, mods, l, seq, tr, 3, 5)
    return xs[:, :seq]
```

```python
import functools
import math

import numpy as np
import jax
import jax.numpy as jnp
from jax import lax
from jax.experimental import pallas as pl
from jax.experimental.pallas import tpu as pltpu

F32 = jnp.float32
BF16 = jnp.bfloat16
HIGHEST = lax.Precision.HIGHEST

NORM_EPS = 1e-6
N_MOD = 6
GRID_W = 64
HY_BANDS = 16
HY_TARGET = 1e-2
HY_FAST_PCT = 0.3
HY_SLOW_PCT = 1.5
RW_HEAD = 64
RW_DECAY_SCALE = math.exp(-0.5)
RW_GN_EPS = 64e-5
HG_HEAD = 128
HG_CHUNK = 64
AT_HEAD = 128
AT_KV_HEADS = 4
ROPE_THETA = 10000.0

LANES = 128
SUBLANES = 8
VMEM_LIMIT = 56 << 20


def _params(*sem):
    return pltpu.CompilerParams(dimension_semantics=sem, vmem_limit_bytes=VMEM_LIMIT)


def _dot(a, b):
    return jnp.dot(a.astype(BF16), b.astype(BF16), preferred_element_type=F32)


def _dot_f32(a, b):
    return jnp.dot(a, b, precision=HIGHEST, preferred_element_type=F32)


def _dot_nt(a, b, exact=False):
    dn = (((1,), (1,)), ((), ()))
    if exact:
        return lax.dot_general(a, b, dn, precision=HIGHEST, preferred_element_type=F32)
    return lax.dot_general(a.astype(BF16), b.astype(BF16), dn, preferred_element_type=F32)


def _dot_tn(a, b, exact=False):
    dn = (((0,), (0,)), ((), ()))
    if exact:
        return lax.dot_general(a, b, dn, precision=HIGHEST, preferred_element_type=F32)
    return lax.dot_general(a.astype(BF16), b.astype(BF16), dn, preferred_element_type=F32)


def _dot_sel(sel, x):
    hi = x.astype(BF16)
    rest = x - hi.astype(F32)
    mid = rest.astype(BF16)
    lo = (rest - mid.astype(F32)).astype(BF16)
    s = sel.astype(BF16)
    dot = lambda p: jnp.dot(s, p, preferred_element_type=F32)
    return dot(hi) + dot(mid) + dot(lo)


def _sigmoid(x):
    return jax.nn.sigmoid(x)


def _pick(n, prefs):
    for p in prefs:
        if n % p == 0:
            return p
    return n


def _adaln_kernel(cond_ref, down_ref, up_ref, bias_ref, o_ref):
    cnd = cond_ref[...]
    t = _dot_f32(cnd * _sigmoid(cnd), down_ref[...])
    o_ref[...] = _dot_f32(t, up_ref[...]) + bias_ref[...]


def _adaln(cond8, down, up, bias):
    depth, d, r = down.shape
    n = up.shape[2]
    tn = _pick(n, (2048, 1024, 512, 256, 128))
    return pl.pallas_call(
        _adaln_kernel,
        grid=(depth, n // tn),
        in_specs=[
            pl.BlockSpec((SUBLANES, d), lambda l, j: (0, 0)),
            pl.BlockSpec((None, d, r), lambda l, j: (l, 0, 0)),
            pl.BlockSpec((None, r, tn), lambda l, j: (l, 0, j)),
            pl.BlockSpec((None, 1, tn), lambda l, j: (l, 0, j)),
        ],
        out_specs=pl.BlockSpec((None, SUBLANES, tn), lambda l, j: (l, 0, j)),
        out_shape=jax.ShapeDtypeStruct((depth, SUBLANES, n), F32),
        compiler_params=_params("arbitrary", "arbitrary"),
        name="adaln",
    )(cond8, down, up, bias.reshape(depth, 1, n))


def _rms(x):
    return x * lax.rsqrt(jnp.mean(x * x, axis=-1, keepdims=True) + NORM_EPS)


def _norm_mod_kernel(x_ref, g_ref, mod_ref, o_ref, *, g_row, shift_row, scale_row):
    y = _rms(x_ref[...]) * g_ref[g_row:g_row + 1, :]
    y = y * (1.0 + mod_ref[scale_row:scale_row + 1, :]) + mod_ref[shift_row:shift_row + 1, :]
    o_ref[...] = y.astype(o_ref.dtype)


def _resid_kernel(x_ref, y_ref, g_ref, mod_ref, o_ref, *, g_row, gate_row):
    yn = _rms(y_ref[...]) * g_ref[g_row:g_row + 1, :]
    o_ref[...] = x_ref[...] + mod_ref[gate_row:gate_row + 1, :] * yn


def _row_specs(seq, tr, d, layer):
    n_lat = seq // tr
    tok = pl.BlockSpec((None, tr, d), lambda b, i: (b, i, 0))
    g = pl.BlockSpec((None, 4, d), lambda b, i: (layer, 0, 0))
    mod = pl.BlockSpec((None, None, None, N_MOD, d),
                       lambda b, i: (layer, b, jnp.where(i < n_lat, 0, 1), 0, 0))
    return tok, g, mod


def _norm_mod(xs, norm_g, mods, layer, seq, tr, g_row, shift_row, scale_row):
    b, t, d = xs.shape
    tok, g, mod = _row_specs(seq, tr, d, layer)
    return pl.pallas_call(
        functools.partial(_norm_mod_kernel, g_row=g_row, shift_row=shift_row, scale_row=scale_row),
        grid=(b, t // tr),
        in_specs=[tok, g, mod],
        out_specs=tok,
        out_shape=jax.ShapeDtypeStruct((b, t, d), BF16),
        compiler_params=_params("parallel", "parallel"),
        name="norm_mod",
    )(xs, norm_g, mods)


def _resid(xs, y, norm_g, mods, layer, seq, tr, g_row, gate_row):
    b, t, d = xs.shape
    tok, g, mod = _row_specs(seq, tr, d, layer)
    return pl.pallas_call(
        functools.partial(_resid_kernel, g_row=g_row, gate_row=gate_row),
        grid=(b, t // tr),
        in_specs=[tok, tok, g, mod],
        out_specs=tok,
        out_shape=jax.ShapeDtypeStruct((b, t, d), F32),
        compiler_params=_params("parallel", "parallel"),
        name="resid",
    )(xs, y.reshape(b, t, d), norm_g, mods)


def _mm_kernel(*refs, n_in):
    a_refs, w_refs, o_ref = refs[:n_in], refs[n_in:2 * n_in], refs[2 * n_in]
    acc = _dot(a_refs[0][...], w_refs[0][...])
    for a_ref, w_ref in zip(a_refs[1:], w_refs[1:]):
        acc = acc + _dot(a_ref[...], w_ref[...])
    o_ref[...] = acc.astype(o_ref.dtype)


def _mm(a_list, w, col0, ncols, out_dtype, tm=1536, tn=512):
    m = a_list[0].shape[0]
    tm = _pick(m, (tm, 512, 256, 128))
    tn = next(t for t in (tn, 256, 128) if ncols % t == 0 and col0 % t == 0)
    n_in = len(a_list)
    kw = a_list[0].shape[1]
    a_specs = [pl.BlockSpec((tm, kw), lambda i, j: (i, 0)) for _ in a_list]
    w_specs = [pl.BlockSpec((kw, tn), functools.partial(lambda i, j, r: (r, col0 // tn + j), r=r))
               for r in range(n_in)]
    return pl.pallas_call(
        functools.partial(_mm_kernel, n_in=n_in),
        grid=(m // tm, ncols // tn),
        in_specs=a_specs + w_specs,
        out_specs=pl.BlockSpec((tm, tn), lambda i, j: (i, j)),
        out_shape=jax.ShapeDtypeStruct((m, ncols), out_dtype),
        compiler_params=_params("parallel", "parallel"),
        name="proj",
    )(*a_list, *([w] * n_in))


def _swiglu_kernel(a_ref, w1_ref, w3_ref, o_ref):
    a = a_ref[...]
    g = _dot(a, w1_ref[...])
    u = _dot(a, w3_ref[...])
    o_ref[...] = (g * _sigmoid(g) * u).astype(o_ref.dtype)


def _swiglu_up(a, w1, w3, tm=1536, tn=256):
    m, k = a.shape
    n = w1.shape[1]
    tm = _pick(m, (tm, 512, 256, 128))
    tn = _pick(n, (tn, 128))
    return pl.pallas_call(
        _swiglu_kernel,
        grid=(m // tm, n // tn),
        in_specs=[pl.BlockSpec((tm, k), lambda i, j: (i, 0)),
                  pl.BlockSpec((k, tn), lambda i, j: (0, j)),
                  pl.BlockSpec((k, tn), lambda i, j: (0, j))],
        out_specs=pl.BlockSpec((tm, tn), lambda i, j: (i, j)),
        out_shape=jax.ShapeDtypeStruct((m, n), BF16),
        compiler_params=_params("parallel", "parallel"),
        name="swiglu_up",
    )(a, w1, w3)


def _mm_acc_kernel(a_ref, w_ref, o_ref, acc_ref):
    k = pl.program_id(2)

    @pl.when(k == 0)
    def _():
        acc_ref[...] = jnp.zeros_like(acc_ref)

    acc_ref[...] += _dot(a_ref[...], w_ref[...])

    @pl.when(k == pl.num_programs(2) - 1)
    def _():
        o_ref[...] = acc_ref[...]


def _mm_ksplit(a, w, tm=768, tn=1024):
    m, kdim = a.shape
    n = w.shape[1]
    tm = _pick(m, (tm, 512, 256, 128))
    tn = _pick(n, (tn, 512, 256, 128))
    tk = kdim // 2 if (kdim // 2) % LANES == 0 else kdim
    return pl.pallas_call(
        _mm_acc_kernel,
        grid=(m // tm, n // tn, kdim // tk),
        in_specs=[pl.BlockSpec((tm, tk), lambda i, j, k: (i, k)),
                  pl.BlockSpec((tk, tn), lambda i, j, k: (k, j))],
        out_specs=pl.BlockSpec((tm, tn), lambda i, j, k: (i, j)),
        out_shape=jax.ShapeDtypeStruct((m, n), F32),
        scratch_shapes=[pltpu.VMEM((tm, tn), F32)],
        compiler_params=_params("parallel", "parallel", "arbitrary"),
        name="ffn_down",
    )(a, w)


def _neighbours(x, seq):
    t = x.shape[0]
    row = lax.broadcasted_iota(jnp.int32, x.shape, 0)
    prev = jnp.where(row == 0, 0.0, jnp.where(row == seq, 0.0, pltpu.roll(x, 1, 0)))
    nxt = jnp.where(row == seq - 1, 0.0, jnp.where(row == t - 1, 0.0, pltpu.roll(x, t - 1, 0)))
    return prev, nxt


def _short_conv_kernel(z_ref, p_ref, o_ref, *, seq):
    z = z_ref[...]
    prev, nxt = _neighbours(z, seq)
    o_ref[...] = p_ref[0:1, :] * prev + p_ref[1:2, :] * z + p_ref[2:3, :] * nxt + p_ref[3:4, :]


def _token_shift_kernel(z_ref, p_ref, o_ref, *, seq):
    z = z_ref[...]
    prev, nxt = _neighbours(z, seq)
    o_ref[...] = z + p_ref[...] * (0.5 * (prev + nxt) - z)


def _time_mix(body, z, params, seq, name):
    b, t, cw = z.shape
    tc = LANES
    return pl.pallas_call(
        functools.partial(body, seq=seq),
        grid=(b, cw // tc),
        in_specs=[pl.BlockSpec((None, t, tc), lambda i, j: (i, 0, j)),
                  pl.BlockSpec((params.shape[0], tc), lambda i, j: (0, j))],
        out_specs=pl.BlockSpec((None, t, tc), lambda i, j: (i, 0, j)),
        out_shape=jax.ShapeDtypeStruct((b, t, cw), F32),
        compiler_params=_params("parallel", "parallel"),
        name=name,
    )(z, params)


def _hy_hidden_kernel(bands_ref, w1t_ref, w1c_ref, w1s_ref, b1_ref, w2_ref, b2_ref, freq_ref, o_ref, *, length):
    hp = o_ref.shape[1]
    pos_b = lax.broadcasted_iota(jnp.int32, (length, bands_ref.shape[1]), 0).astype(F32)
    ang = (2 * math.pi) * pos_b / length * bands_ref[...]
    t = lax.broadcasted_iota(jnp.int32, (length, hp), 0).astype(F32) / max(length - 1, 1)
    pre = (t * w1t_ref[...] + _dot_f32(jnp.cos(ang), w1c_ref[...]) + _dot_f32(-jnp.sin(ang), w1s_ref[...])
           + b1_ref[...])
    h = jnp.sin(freq_ref[0:1, :] * pre)
    o_ref[...] = jnp.sin(freq_ref[1:2, :] * (_dot_f32(h, w2_ref[...]) + b2_ref[...]))


def _hy_hidden(length, bands, w1t, w1c, w1s, b1, w2, b2, freq):
    hp = w2.shape[0]
    full = lambda a: pl.BlockSpec(a.shape, lambda i: (0,) * a.ndim)
    args = (bands, w1t, w1c, w1s, b1, w2, b2, freq)
    return pl.pallas_call(
        functools.partial(_hy_hidden_kernel, length=length),
        grid=(1,),
        in_specs=[full(a) for a in args],
        out_specs=pl.BlockSpec((length, hp), lambda i: (0, 0)),
        out_shape=jax.ShapeDtypeStruct((length, hp), F32),
        compiler_params=_params("arbitrary"),
        name="hy_hidden",
    )(*args)


def _hy_taps_kernel(h_ref, w3_ref, delta_ref, o_ref, *, length):
    t = lax.broadcasted_iota(jnp.int32, o_ref.shape, 0).astype(F32) / max(length - 1, 1)
    o_ref[...] = _dot_f32(h_ref[...], w3_ref[...]) * jnp.exp(-t * delta_ref[...])


def _hy_taps(hmid, w3p, delta, c):
    length, hp = hmid.shape
    tc = _pick(c, (256, 128))
    nct = c // tc
    return pl.pallas_call(
        functools.partial(_hy_taps_kernel, length=length),
        grid=(4, nct),
        in_specs=[pl.BlockSpec((length, hp), lambda g, j: (0, 0)),
                  pl.BlockSpec((hp, tc), lambda g, j: (0, g * nct + j)),
                  pl.BlockSpec((1, tc), lambda g, j: (0, j))],
        out_specs=pl.BlockSpec((None, length, tc), lambda g, j: (g, 0, j)),
        out_shape=jax.ShapeDtypeStruct((4, length, c), F32),
        compiler_params=_params("parallel", "parallel"),
        name="hy_taps",
    )(hmid, w3p, delta)


def _fft_tables(n1, n2):
    n = n1 * n2
    n1h = n1 // 2

    def cs(phase, period):
        ang = (phase % period).astype(F32) * (2 * math.pi / period)
        return jnp.cos(ang), jnp.sin(ang)

    k1 = jnp.arange(n1, dtype=jnp.int32)
    c, s = cs(k1[:, None] * jnp.arange(n1h, dtype=jnp.int32)[None, :], n1)
    f_a = jnp.concatenate([c, -s], axis=0)
    c, s = cs(jnp.arange(n1h, dtype=jnp.int32)[:, None] * k1[None, :], n1)
    f_c = jnp.concatenate([c, -s], axis=1) / n
    k2 = jnp.arange(n2, dtype=jnp.int32)
    freq = k1[:, None, None] + n1 * k2[None, :, None]
    c, s = cs(freq * k2[None, None, :], n)
    g = jnp.concatenate([jnp.concatenate([c, s], axis=2),
                         jnp.concatenate([-s, c], axis=2)], axis=1)
    ct, st = jnp.swapaxes(c, 1, 2), jnp.swapaxes(s, 1, 2)
    gh = jnp.concatenate([jnp.concatenate([ct, -st], axis=2),
                          jnp.concatenate([st, ct], axis=2)], axis=1)
    return f_a.astype(BF16), f_c.astype(BF16), g.astype(BF16), gh.astype(BF16)


def _fft_a_kernel(u_ref, f_ref, ar_ref, ai_ref):
    n1 = ar_ref.shape[0]
    res = _dot(f_ref[...], u_ref[...])
    ar_ref[...] = res[:n1].astype(BF16)
    ai_ref[...] = res[n1:].astype(BF16)


def _merged(groups, g):
    return lambda rows, c: pl.BlockSpec((None, rows, c), lambda b, j: (b, 0, j * groups + g))


def _split(g):
    return lambda rows, c: pl.BlockSpec((None, None, rows, c), lambda b, j: (b, j, 0, g))


def _fft_a(src, layout, c, n2, f_a):
    bn = src.shape[0]
    n1 = f_a.shape[0] // 2
    out = jax.ShapeDtypeStruct((bn, n1, n2 * c), BF16)
    ospec = pl.BlockSpec((None, n1, c), lambda b, j: (b, 0, j))
    ar, ai = pl.pallas_call(
        _fft_a_kernel,
        grid=(bn, n2),
        in_specs=[layout(n1 // 2, c), pl.BlockSpec(f_a.shape, lambda b, j: (0, 0))],
        out_specs=[ospec, ospec],
        out_shape=[out, out],
        compiler_params=_params("parallel", "parallel"),
        name="fft_a",
    )(src, f_a)
    return ar.reshape(bn, n1, n2, c), ai.reshape(bn, n1, n2, c)


def _stack(r_ref, i_ref):
    return jnp.concatenate([r_ref[...], i_ref[...]], axis=0)


def _fft_spec_kernel(fr_ref, fi_ref, br_ref, bi_ref, g_ref, kr_ref, ki_ref):
    n2 = kr_ref.shape[0]
    xf = jnp.dot(g_ref[...], _stack(fr_ref, fi_ref), preferred_element_type=F32)
    xb = jnp.dot(g_ref[...], _stack(br_ref, bi_ref), preferred_element_type=F32)
    kr_ref[...] = xf[:n2] + xb[:n2]
    ki_ref[...] = xf[n2:] - xb[n2:]


def _fft_spec(tr, ti, g):
    _, n1, n2, c = tr.shape
    fwd = pl.BlockSpec((None, None, n2, c), lambda o, k: (o, k, 0, 0))
    bwd = pl.BlockSpec((None, None, n2, c), lambda o, k: (2 + o, k, 0, 0))
    out = jax.ShapeDtypeStruct((2, n1, n2, c), F32)
    return pl.pallas_call(
        _fft_spec_kernel,
        grid=(2, n1),
        in_specs=[fwd, fwd, bwd, bwd, pl.BlockSpec((None, 2 * n2, 2 * n2), lambda o, k: (k, 0, 0))],
        out_specs=[fwd, fwd],
        out_shape=[out, out],
        compiler_params=_params("parallel", "parallel"),
        name="fft_spec",
    )(tr, ti, tr, ti, g)


def _fft_mid_kernel(ar_ref, ai_ref, g_ref, gh_ref, kr_ref, ki_ref, zr_ref, zi_ref):
    n2 = zr_ref.shape[0]
    x = jnp.dot(g_ref[...], _stack(ar_ref, ai_ref), preferred_element_type=F32)
    xr, xi = x[:n2], x[n2:]
    kr, ki = kr_ref[...], ki_ref[...]
    y = jnp.concatenate([(xr * kr - xi * ki).astype(BF16), (xr * ki + xi * kr).astype(BF16)], axis=0)
    z = jnp.dot(gh_ref[...], y, preferred_element_type=F32)
    zr_ref[...] = z[:n2].astype(BF16)
    zi_ref[...] = z[n2:].astype(BF16)


def _fft_mid(ar, ai, g, gh, kr, ki, order):
    bn, n1, n2, c = ar.shape
    act = pl.BlockSpec((None, None, n2, c), lambda k, b: (b, k, 0, 0))
    mat = pl.BlockSpec((None, 2 * n2, 2 * n2), lambda k, b: (k, 0, 0))
    spec = pl.BlockSpec((None, None, n2, c), lambda k, b: (order, k, 0, 0))
    out = jax.ShapeDtypeStruct((bn, n1, n2, c), BF16)
    return pl.pallas_call(
        _fft_mid_kernel,
        grid=(n1, bn),
        in_specs=[act, act, mat, mat, spec, spec],
        out_specs=[act, act],
        out_shape=[out, out],
        compiler_params=_params("parallel", "parallel"),
        name="fft_mid",
    )(ar, ai, g, gh, kr, ki)


def _fft_c_kernel(zr_ref, zi_ref, f_ref, u_ref, gate_ref, bias_ref, o_ref):
    conv = jnp.dot(f_ref[...], _stack(zr_ref, zi_ref), preferred_element_type=F32)
    u = u_ref[...]
    o_ref[...] = gate_ref[...] * (conv + bias_ref[...] * u)


def _fft_c(zr, zi, f_c, u, u_layout, gate, gate_layout, bias, order):
    bn, n1, n2, c = zr.shape
    zspec = pl.BlockSpec((None, n1, c), lambda b, j: (b, 0, j))
    return pl.pallas_call(
        _fft_c_kernel,
        grid=(bn, n2),
        in_specs=[zspec, zspec, pl.BlockSpec(f_c.shape, lambda b, j: (0, 0)), u_layout(n1 // 2, c),
                  gate_layout(n1 // 2, c), pl.BlockSpec((None, 1, c), lambda b, j: (order, 0, 0))],
        out_specs=_merged(1, 0)(n1 // 2, c),
        out_shape=jax.ShapeDtypeStruct((bn, n1 // 2, n2 * c), F32),
        compiler_params=_params("parallel", "parallel"),
        name="fft_c",
    )(zr.reshape(bn, n1, n2 * c), zi.reshape(bn, n1, n2 * c), f_c, u, gate, bias)


def _hy_ctx_kernel(x1_ref, x2_ref, v_ref, taps_ref, bias_ref, fwd_ref, inv_ref, o_ref):
    n = fwd_ref.shape[0] // 2
    fwd, inv = fwd_ref[...], inv_ref[...]

    def spec(a):
        s = _dot_f32(fwd, a)
        return s[:n], s[n:]

    u = v_ref[...]
    for o, gate_ref in ((0, x1_ref), (1, x2_ref)):
        fr, fi = spec(taps_ref[o])
        br, bi = spec(taps_ref[2 + o])
        kr, ki = fr + br, fi - bi
        ur, ui = spec(u)
        y = jnp.concatenate([ur * kr - ui * ki, ur * ki + ui * kr], axis=0)
        u = gate_ref[...] * (_dot_f32(inv, y) + bias_ref[o:o + 1, :] * u)
    o_ref[...] = u


def _hy_ctx(zc, taps, bias, seq, ctx_len, c):
    b = zc.shape[0]
    n = 2 * ctx_len
    tc = _pick(c, (256, 128))
    nct = c // tc
    k = jnp.arange(n, dtype=jnp.int32)[:, None]
    m = jnp.arange(ctx_len, dtype=jnp.int32)[None, :]
    ang = ((k * m) % n).astype(F32) * (2 * math.pi / n)
    fwd = jnp.concatenate([jnp.cos(ang), -jnp.sin(ang)], axis=0)
    inv = jnp.concatenate([jnp.cos(ang).T, -jnp.sin(ang).T], axis=1) / n
    row = seq // ctx_len
    tok = lambda g: pl.BlockSpec((None, ctx_len, tc), lambda j, i: (i, row, g * nct + j))
    return pl.pallas_call(
        _hy_ctx_kernel,
        grid=(nct, b),
        in_specs=[tok(0), tok(1), tok(2),
                  pl.BlockSpec((4, ctx_len, tc), lambda j, i: (0, 0, j)),
                  pl.BlockSpec((2, tc), lambda j, i: (0, j)),
                  pl.BlockSpec(fwd.shape, lambda j, i: (0, 0)),
                  pl.BlockSpec(inv.shape, lambda j, i: (0, 0))],
        out_specs=pl.BlockSpec((None, ctx_len, tc), lambda j, i: (i, 0, j)),
        out_shape=jax.ShapeDtypeStruct((b, ctx_len, c), F32),
        compiler_params=_params("parallel", "parallel"),
        name="hy_ctx",
    )(zc, zc, zc, taps, bias, fwd, inv)


def _hyena(zh, hy, seq, ctx_len, tables):
    short_w, short_b, w1, b1, w2, b2, freq, w3, bias = hy
    b, t, c3 = zh.shape
    c = c3 // 3
    f_a, f_c, g, gh = tables
    n2 = LANES
    zc = _time_mix(_short_conv_kernel, zh, jnp.concatenate([short_w, short_b[None]], axis=0), seq, "short_conv")

    hf = w2.shape[0]
    hp = LANES
    pad_c = lambda a: jnp.pad(a, ((0, 0), (0, hp - hf)))
    bands = jnp.pad(jnp.linspace(1e-4, HY_BANDS - 1, HY_BANDS, dtype=F32)[None, :], ((0, 0), (0, LANES - HY_BANDS)))
    w1c = jnp.pad(w1[1:1 + HY_BANDS], ((0, LANES - HY_BANDS), (0, hp - hf)))
    w1s = jnp.pad(w1[1 + HY_BANDS:], ((0, LANES - HY_BANDS), (0, hp - hf)))
    hid_args = (bands, pad_c(w1[0:1]), w1c, w1s, pad_c(b1[None]), jnp.pad(w2, ((0, hp - hf), (0, hp - hf))),
                pad_c(b2[None]), pad_c(freq))
    w3p = jnp.pad(w3, ((0, hp - hf), (0, 0)))
    max_decay = math.log(HY_TARGET) / HY_FAST_PCT
    min_decay = math.log(HY_TARGET) / HY_SLOW_PCT
    delta = jnp.abs(jnp.linspace(min_decay, max_decay, c, dtype=F32))[None, :]

    taps_l = _hy_taps(_hy_hidden(seq, *hid_args), w3p, delta, c)
    taps_c = _hy_taps(_hy_hidden(ctx_len, *hid_args), w3p, delta, c)

    as_split = lambda a: jnp.swapaxes(a.reshape(a.shape[0], seq // n2, n2, a.shape[-1]), 1, 2)
    tr, ti = _fft_a(as_split(taps_l), _split(0), c, n2, f_a)
    kr, ki = _fft_spec(tr, ti, g)

    zct = as_split(zc[:, :seq])
    bias3 = bias.reshape(2, 1, c)
    ar, ai = _fft_a(zct, _split(2), c, n2, f_a)
    zr, zi = _fft_mid(ar, ai, g, gh, kr, ki, 0)
    u1 = _fft_c(zr, zi, f_c, zct, _split(2), zct, _split(0), bias3, 0)
    ar, ai = _fft_a(u1, _merged(1, 0), c, n2, f_a)
    zr, zi = _fft_mid(ar, ai, g, gh, kr, ki, 1)
    u2 = _fft_c(zr, zi, f_c, u1, _merged(1, 0), zct, _split(1), bias3, 1)
    out_c = _hy_ctx(zc, taps_c, bias, seq, ctx_len, c)
    return jnp.concatenate([u2.reshape(b, seq, c), out_c], axis=1)


def _head_sum(x, n_heads):
    fold = x[:, :LANES]
    for s in range(LANES, x.shape[1], LANES):
        fold = fold + x[:, s:s + LANES]
    r = lax.broadcasted_iota(jnp.int32, (LANES, LANES), 0) & (n_heads - 1)
    c = lax.broadcasted_iota(jnp.int32, (LANES, LANES), 1) & (n_heads - 1)
    per_head = _dot_f32(fold, jnp.where(r == c, 1.0, 0.0))
    return jnp.concatenate([per_head] * (x.shape[1] // LANES), axis=1)


def _rw_feat_kernel(r_ref, k_ref, v_ref, zl_ref, wup_ref, aup_ref, gup_ref, w0_ref, a0_ref, kkp_ref, kap_ref,
                    rk_ref, kk_o, g_o, bonus_o, w_o, kd_o, kka_o, *, n_heads):
    r, k, v = r_ref[...], k_ref[...], v_ref[...]
    zl = zl_ref[...]
    wl, al, gl = zl[:, :LANES], zl[:, LANES:2 * LANES], zl[:, 2 * LANES:]
    kq = k * kkp_ref[...]
    kk = kq * lax.rsqrt(_head_sum(kq * kq, n_heads) + 1e-12)
    kk_o[...] = kk
    g_o[...] = _dot(_sigmoid(gl), gup_ref[...])
    twl = jnp.tanh(wl)
    bonus = jnp.zeros_like(v)
    for d in range(2):
        w = jnp.exp(-RW_DECAY_SCALE * _sigmoid(w0_ref[d:d + 1, :] + _dot_f32(twl, wup_ref[d])))
        a = _sigmoid(a0_ref[d:d + 1, :] + _dot_f32(al, aup_ref[d]))
        kd = k * (1.0 + (a - 1.0) * kap_ref[...])
        w_o[d] = w
        kd_o[d] = kd
        kka_o[d] = kk * a
        bonus = bonus + _head_sum(r * kd * rk_ref[...], n_heads) * v
    bonus_o[...] = bonus


def _rw_features(zs, zl, w_up, a_up, g_up, w0, a0, k_k, k_a, r_k, tr):
    b, t, c3 = zs.shape
    c = c3 // 3
    tok = lambda g: pl.BlockSpec((None, tr, c), lambda i, j: (i, j, g))
    full = lambda a: pl.BlockSpec(a.shape, lambda i, j: (0,) * a.ndim)
    two = pl.BlockSpec((2, None, tr, c), lambda i, j: (0, i, j, 0))
    s1 = jax.ShapeDtypeStruct((b, t, c), F32)
    s2 = jax.ShapeDtypeStruct((2, b, t, c), F32)
    consts = (w_up, a_up, g_up, w0, a0, k_k, k_a, r_k)
    return pl.pallas_call(
        functools.partial(_rw_feat_kernel, n_heads=c // RW_HEAD),
        grid=(b, t // tr),
        in_specs=[tok(0), tok(1), tok(2), pl.BlockSpec((None, tr, zl.shape[2]), lambda i, j: (i, j, 0))]
        + [full(a) for a in consts],
        out_specs=[tok(0), tok(0), tok(0), two, two, two],
        out_shape=[s1, s1, s1, s2, s2, s2],
        compiler_params=_params("parallel", "parallel"),
        name="rw_features",
    )(zs, zs, zs, zl, *consts)


RW_STEPS = 32
RW_PARTIALS = 4
RW_UNROLL = 2


def _replicated_row(ref, o, b, r):
    return ref[o, b, pl.ds(r, SUBLANES, stride=0), :]


def _rw_scan_kernel(*refs, reverse, n_heads, accumulate):
    if accumulate:
        r_ref, w_ref, kd_ref, v_ref, kk_ref, kka_ref, yin_ref, y_ref, s_ref, rows_ref = refs
    else:
        r_ref, w_ref, kd_ref, v_ref, kk_ref, kka_ref, y_ref, s_ref, rows_ref = refs
    n_batch, n_tiles = s_ref.shape[:2]
    steps = v_ref.shape[1]

    @pl.when(pl.program_id(0) == 0)
    def _():
        s_ref[...] = jnp.zeros_like(s_ref)

    operands = (kk_ref, kka_ref, kd_ref, w_ref, r_ref)
    for o, ref in enumerate(operands):
        for b in range(n_batch):
            for j in range(n_tiles):
                rows_ref[o, b, pl.ds(j, steps, stride=n_tiles), :] = ref[b, :, j * LANES:(j + 1) * LANES]
    KK, KKA, KD, W, R = range(len(operands))

    def total(parts):
        while len(parts) > 1:
            parts = [a + b for a, b in zip(parts[::2], parts[1::2])]
        x = parts[0].reshape(RW_HEAD, LANES)
        shift = LANES // 2
        while shift >= n_heads:
            x = x + pltpu.roll(x, shift, 1)
            shift //= 2
        return x.reshape(parts[0].shape)

    def accumulate_into(parts, j, term):
        parts[j % RW_PARTIALS] = term if parts[j % RW_PARTIALS] is None else parts[j % RW_PARTIALS] + term

    def one_step(b, t):
        row = lambda o, j: _replicated_row(rows_ref, o, b, t * n_tiles + j)[None]
        parts = [None] * RW_PARTIALS
        for j in range(n_tiles):
            accumulate_into(parts, j, s_ref[b, j] * row(KK, j))
        removed = total(parts)
        v = v_ref[b, t].reshape(removed.shape)
        parts = [None] * RW_PARTIALS
        for j in range(n_tiles):
            s = s_ref[b, j] * row(W, j) - removed * row(KKA, j) + v * row(KD, j)
            s_ref[b, j] = s
            accumulate_into(parts, j, s * row(R, j))
        y = total(parts).reshape(v_ref.shape[2:])
        y_ref[b, t] = (yin_ref[b, t] + y) if accumulate else y

    def group(gi, carry):
        for n in range(RW_UNROLL):
            i = gi * RW_UNROLL + n
            t = (steps - 1 - i) if reverse else i
            for b in range(n_batch):
                one_step(b, t)
        return carry

    lax.fori_loop(0, steps // RW_UNROLL, group, 0)


def _segment_order(n_lat, n_ctx, reverse):
    if reverse:
        return lambda j: n_lat + n_ctx - 1 - j
    return lambda j: jnp.where(j < n_ctx, n_lat + j, j - n_ctx)


def _rw_scan(zs, w, kd, vt, kk, kka, seq, direction, y_prev):
    b, t, c = kk.shape
    reverse = direction == 1
    order = _segment_order(seq // RW_STEPS, (t - seq) // RW_STEPS, reverse)
    tok = pl.BlockSpec((b, RW_STEPS, c), lambda j: (0, order(j), 0))
    dtok = pl.BlockSpec((None, b, RW_STEPS, c), lambda j: (direction, 0, order(j), 0))
    val = pl.BlockSpec((b, RW_STEPS, RW_HEAD, LANES), lambda j: (0, order(j), 0, 0))
    accumulate = y_prev is not None
    args = (zs, w, kd, vt, kk, kka) + ((y_prev,) if accumulate else ())
    n_tiles = c // LANES
    return pl.pallas_call(
        functools.partial(_rw_scan_kernel, reverse=reverse, n_heads=c // RW_HEAD, accumulate=accumulate),
        grid=(t // RW_STEPS,),
        in_specs=[tok, dtok, dtok, val, tok, dtok] + ([val] if accumulate else []),
        out_specs=val,
        out_shape=jax.ShapeDtypeStruct(vt.shape, F32),
        scratch_shapes=[pltpu.VMEM((b, n_tiles, RW_HEAD // SUBLANES, SUBLANES, LANES), F32),
                        pltpu.VMEM((5, b, RW_STEPS * n_tiles, LANES), F32)],
        input_output_aliases={6: 0} if accumulate else {},
        compiler_params=_params("arbitrary"),
        name="rw_scan",
    )(*args)


def _rw_out_kernel(y_ref, bonus_ref, g_ref, lnw_ref, lnb_ref, o_ref, *, n_heads):
    y = y_ref[...]
    mean = _head_sum(y, n_heads) * (1.0 / RW_HEAD)
    yc = y - mean
    var = _head_sum(yc * yc, n_heads) * (1.0 / RW_HEAD)
    yn = yc * lax.rsqrt(var + RW_GN_EPS) * lnw_ref[...] + lnb_ref[...]
    o_ref[...] = ((yn + bonus_ref[...]) * g_ref[...]).astype(o_ref.dtype)


def _rw_out(y, bonus, g, ln_w, ln_b, tr):
    b, t, c = y.shape
    tok = pl.BlockSpec((None, tr, c), lambda i, j: (i, j, 0))
    vec = pl.BlockSpec((1, c), lambda i, j: (0, 0))
    return pl.pallas_call(
        functools.partial(_rw_out_kernel, n_heads=c // RW_HEAD),
        grid=(b, t // tr),
        in_specs=[tok, tok, tok, vec, vec],
        out_specs=tok,
        out_shape=jax.ShapeDtypeStruct((b, t, c), BF16),
        compiler_params=_params("parallel", "parallel"),
        name="rw_out",
    )(y, bonus, g, ln_w, ln_b)


def _pad_rows(a, rows):
    return jnp.pad(a, [(0, 0)] * (a.ndim - 2) + [(0, rows - a.shape[-2]), (0, 0)])


def _rw_perm(c):
    return jnp.arange(c).reshape(c // RW_HEAD, RW_HEAD).T.reshape(-1)


def _rwkv(zs_raw, zl_raw, rw, seq, tr):
    mu, w0, w_up, a0, a_up, g_up, k_k, k_a, r_k, ln_w, ln_b = rw
    b, t, _ = zs_raw.shape
    c = k_k.shape[0]
    n_heads = c // RW_HEAD
    perm = _rw_perm(c)
    pc = lambda a: jnp.take(a, perm, axis=-1)
    lw, la = w_up.shape[1], a_up.shape[1]
    mu_l = mu[3 * c:]
    pad_to = lambda a: jnp.pad(a, (0, LANES - a.shape[0]))
    mu_lora = jnp.concatenate([pad_to(mu_l[:lw]), pad_to(mu_l[lw:lw + la]), mu_l[lw + la:]])
    mu_main = pc(mu[:3 * c].reshape(3, c)).reshape(1, 3 * c)
    zs = _time_mix(_token_shift_kernel, zs_raw, mu_main, seq, "token_shift")
    zl = _time_mix(_token_shift_kernel, zl_raw, mu_lora[None, :], seq, "token_shift")
    kk, g, bonus, w, kd, kka = _rw_features(
        zs, zl, _pad_rows(pc(w_up), LANES), _pad_rows(pc(a_up), LANES), pc(g_up).astype(BF16), pc(w0), pc(a0),
        pc(k_k)[None], pc(k_a)[None], pc(r_k.reshape(c))[None], min(tr, 128))

    vt = jnp.tile(zs[..., 2 * c:].reshape(b, t, RW_HEAD, n_heads), (1, 1, 1, LANES // n_heads))
    y = _rw_scan(zs, w, kd, vt, kk, kka, seq, 0, None)
    y = _rw_scan(zs, w, kd, vt, kk, kka, seq, 1, y)
    y = y[..., :n_heads].reshape(b, t, c)
    return _rw_out(y, bonus, g, pc(ln_w)[None], pc(ln_b)[None], tr)


def _hg_bounds_kernel(x_ref, o_ref):
    x = x_ref[...]
    e = jnp.exp(x - jnp.max(x, axis=0, keepdims=True))
    p = e / jnp.sum(e, axis=0, keepdims=True)
    run = jnp.zeros_like(p[0:1])
    o_ref[0:1, :] = run
    for i in range(1, x.shape[0]):
        run = run + p[i:i + 1]
        o_ref[i:i + 1, :] = run


def _hg_bounds(lower):
    return pl.pallas_call(
        _hg_bounds_kernel,
        out_shape=jax.ShapeDtypeStruct(lower.shape, F32),
        name="hg_bounds",
    )(lower)


HG_HEADS_PER_STEP = 16


def _hg_scan_kernel(q_ref, f_ref, i_ref, lb_ref, o_ref, s_ref, *, reverse):
    cs = q_ref.shape[0]
    n_heads = s_ref.shape[0]

    @pl.when(pl.program_id(2) == 0)
    def _():
        s_ref[...] = jnp.zeros_like(s_ref)

    width = q_ref.shape[1]
    row = lax.broadcasted_iota(jnp.int32, (cs, cs), 0)
    col = lax.broadcasted_iota(jnp.int32, (cs, cs), 1)
    row_w = lax.broadcasted_iota(jnp.int32, (cs, width), 0)
    inclusive = lambda upto: jnp.where((col >= upto) if reverse else (col <= upto), 1.0, 0.0)
    sums = [inclusive(row)]
    levels = []
    hs = cs // 2
    while hs >= 1:
        blk = 2 * hs
        boundary = (row & ~(blk - 1)) + hs
        sums.append(inclusive(boundary if reverse else boundary - 1))
        levels.append(((row & ~(blk - 1)) == (col & ~(blk - 1)), (row_w & (blk - 1)) >= hs))
        hs //= 2

    heads = [slice(h * HG_HEAD, (h + 1) * HG_HEAD) for h in range(n_heads)]
    q, v, lb = q_ref[...], i_ref[...], lb_ref[...]
    f = lb + (1.0 - lb) * _sigmoid(f_ref[...])
    k = 1.0 - f
    bb = _dot_sel(jnp.concatenate(sums, axis=0), jnp.log(f))
    b = bb[:cs]
    qk = q * k
    att = [jnp.where(row == col, jnp.sum(qk[:, sl], axis=-1, keepdims=True), 0.0) for sl in heads]
    for l, (same, late) in enumerate(levels):
        ref_b = bb[(l + 1) * cs:(l + 2) * cs]
        q_side, k_side = (jnp.logical_not(late), late) if reverse else (late, jnp.logical_not(late))
        qt = (q * jnp.exp(jnp.where(q_side, b - ref_b, -jnp.inf))).astype(BF16)
        kt = (k * jnp.exp(jnp.where(k_side, ref_b - b, -jnp.inf))).astype(BF16)
        for h, sl in enumerate(heads):
            att[h] = att[h] + jnp.where(same, _dot_nt(qt[:, sl], kt[:, sl]), 0.0)
    b_end = b[0:1, :] if reverse else b[cs - 1:cs, :]
    qe = (q * jnp.exp(b)).astype(BF16)
    ke = (k * jnp.exp(b_end - b)).astype(BF16)
    decay = jnp.exp(b_end)
    vb = v.astype(BF16)
    for h, sl in enumerate(heads):
        st = s_ref[h]
        o_ref[:, sl] = _dot(att[h], vb[:, sl]) + _dot_nt(qe[:, sl], st)
        s_ref[h] = decay[:, sl] * st + _dot_tn(vb[:, sl], ke[:, sl])


def _hg_scan(p, lb, c, seq, direction):
    b, t, _ = p.shape
    wl = min(HG_HEADS_PER_STEP * HG_HEAD, c)
    ng = c // wl
    reverse = direction == 1
    order = _segment_order(seq // HG_CHUNK, (t - seq) // HG_CHUNK, reverse)
    blk = lambda g: pl.BlockSpec((None, HG_CHUNK, wl), lambda i, n, j: (i, order(j), g * ng + n))
    return pl.pallas_call(
        functools.partial(_hg_scan_kernel, reverse=reverse),
        grid=(b, ng, t // HG_CHUNK),
        in_specs=[blk(0), blk(1 + direction), blk(3), pl.BlockSpec((1, wl), lambda i, n, j: (0, n))],
        out_specs=blk(0),
        out_shape=jax.ShapeDtypeStruct((b, t, c), F32),
        scratch_shapes=[pltpu.VMEM((wl // HG_HEAD, HG_HEAD, HG_HEAD), F32)],
        compiler_params=_params("parallel", "parallel", "arbitrary"),
        name="hg_scan",
    )(p, p, p, lb)


def _hg_out_kernel(o0_ref, o1_ref, g_ref, w_ref, out_ref):
    o = o0_ref[...] + o1_ref[...]
    g = g_ref[...]
    for s in range(0, o.shape[1], HG_HEAD):
        oh = o[:, s:s + HG_HEAD]
        gh = g[:, s:s + HG_HEAD]
        out_ref[:, s:s + HG_HEAD] = (_rms(oh) * w_ref[...] * (gh * _sigmoid(gh))).astype(out_ref.dtype)


def _hg_out(o0, o1, p, norm_w, tr):
    b, t, c = o0.shape
    tc = _pick(c, (512, 256, 128))
    nct = c // tc
    tok = pl.BlockSpec((None, tr, tc), lambda i, j, n: (i, j, n))
    return pl.pallas_call(
        _hg_out_kernel,
        grid=(b, t // tr, nct),
        in_specs=[tok, tok, pl.BlockSpec((None, tr, tc), lambda i, j, n: (i, j, 4 * nct + n)),
                  pl.BlockSpec((1, HG_HEAD), lambda i, j, n: (0, 0))],
        out_specs=tok,
        out_shape=jax.ShapeDtypeStruct((b, t, c), BF16),
        compiler_params=_params("parallel", "parallel", "parallel"),
        name="hg_out",
    )(o0, o1, p, norm_w)


def _rope_tables(seq):
    t = jnp.arange(seq)
    axis = AT_HEAD // 2
    inv = ROPE_THETA ** (-jnp.arange(0, axis, 2, dtype=F32) / axis)
    ang = jnp.concatenate([(t // GRID_W)[:, None] * inv, (t % GRID_W)[:, None] * inv], axis=-1)
    sign = jnp.tile(jnp.array([-1.0, 1.0], F32), AT_HEAD // 2)
    return jnp.repeat(jnp.cos(ang), 2, axis=-1), jnp.repeat(jnp.sin(ang), 2, axis=-1) * sign


def _at_prep_kernel(q_ref, k_ref, qw_ref, kw_ref, cos_ref, sin_ref, qo_ref, ko_ref, *, n_lat):
    is_lat = pl.program_id(1) < n_lat
    cos, sin = cos_ref[...], sin_ref[...]
    even = (lax.broadcasted_iota(jnp.int32, cos.shape, 1) & 1) == 0

    def head(x, w):
        y = _rms(x) * w
        partner = jnp.where(even, pltpu.roll(y, AT_HEAD - 1, 1), pltpu.roll(y, 1, 1))
        return jnp.where(is_lat, y * cos + partner * sin, y)

    scale = AT_HEAD ** -0.5
    for s in range(0, qo_ref.shape[1], AT_HEAD):
        qo_ref[:, s:s + AT_HEAD] = (head(q_ref[:, s:s + AT_HEAD], qw_ref[...]) * scale).astype(qo_ref.dtype)
    for s in range(0, ko_ref.shape[1], AT_HEAD):
        ko_ref[:, s:s + AT_HEAD] = head(k_ref[:, s:s + AT_HEAD], kw_ref[...]).astype(ko_ref.dtype)


def _at_prep(p, c, q_norm, k_norm, cos, sin, seq, tr):
    b, t, _ = p.shape
    kvw = AT_KV_HEADS * AT_HEAD
    n_lat = seq // tr
    tab = pl.BlockSpec((tr, AT_HEAD), lambda i, j: (jnp.minimum(j, n_lat - 1), 0))
    vec = pl.BlockSpec((1, AT_HEAD), lambda i, j: (0, 0))
    return pl.pallas_call(
        functools.partial(_at_prep_kernel, n_lat=n_lat),
        grid=(b, t // tr),
        in_specs=[pl.BlockSpec((None, tr, c), lambda i, j: (i, j, 5)),
                  pl.BlockSpec((None, tr, kvw), lambda i, j: (i, j, 6 * c // kvw)),
                  vec, vec, tab, tab],
        out_specs=[pl.BlockSpec((None, tr, c), lambda i, j: (i, j, 0)),
                   pl.BlockSpec((None, tr, kvw), lambda i, j: (i, j, 0))],
        out_shape=[jax.ShapeDtypeStruct((b, t, c), BF16), jax.ShapeDtypeStruct((b, t, kvw), BF16)],
        compiler_params=_params("parallel", "parallel"),
        name="at_prep",
    )(p, p, q_norm, k_norm, cos, sin)


AT_Q_BLOCK = 128
AT_AHEAD = 3


def _flash_kernel(q_ref, k_ref, v_ref, o_ref, m_ref, l_ref, acc_ref):
    kv = pl.program_id(3)

    @pl.when(kv == 0)
    def _():
        m_ref[...] = jnp.full_like(m_ref, -jnp.inf)
        l_ref[...] = jnp.zeros_like(l_ref)
        acc_ref[...] = jnp.zeros_like(acc_ref)

    k = k_ref[...]
    v = v_ref[...].astype(BF16)
    group, tq, _ = m_ref.shape
    qb = min(AT_Q_BLOCK, tq)
    blocks = [(r, i) for r in range(group) for i in range(tq // qb)]

    def scores(r, i):
        return _dot_nt(q_ref[i * qb:(i + 1) * qb, r * AT_HEAD:(r + 1) * AT_HEAD], k)

    ahead = [scores(*blk) for blk in blocks[:AT_AHEAD]]
    for n, (r, i) in enumerate(blocks):
        s = ahead.pop(0)
        if n + AT_AHEAD < len(blocks):
            ahead.append(scores(*blocks[n + AT_AHEAD]))
        rows = slice(i * qb, (i + 1) * qb)
        m_prev = m_ref[r, rows, :]
        m_cur = jnp.maximum(m_prev, jnp.max(s, axis=-1, keepdims=True))
        alpha = jnp.exp(m_prev - m_cur)
        p = jnp.exp(s - m_cur[:, 0:1])
        l_ref[r, rows, :] = alpha * l_ref[r, rows, :] + jnp.sum(p, axis=-1, keepdims=True)
        acc_ref[r, rows, :] = alpha * acc_ref[r, rows, :] + _dot(p, v)
        m_ref[r, rows, :] = m_cur

    @pl.when(kv == pl.num_programs(3) - 1)
    def _():
        for r in range(group):
            o_ref[:, r * AT_HEAD:(r + 1) * AT_HEAD] = (acc_ref[r] / l_ref[r]).astype(o_ref.dtype)


def _flash(qn, kn, p, c, q_blk0, tq, nq, kv_blk0, tk, nk):
    b = qn.shape[0]
    group = c // AT_HEAD // AT_KV_HEADS
    gw = group * AT_HEAD
    v_col0 = (6 * c + AT_KV_HEADS * AT_HEAD) // AT_HEAD
    return pl.pallas_call(
        _flash_kernel,
        grid=(b, AT_KV_HEADS, nq, nk),
        in_specs=[pl.BlockSpec((None, tq, gw), lambda i, g, a, n: (i, q_blk0 + a, g)),
                  pl.BlockSpec((None, tk, AT_HEAD), lambda i, g, a, n: (i, kv_blk0 + n, g)),
                  pl.BlockSpec((None, tk, AT_HEAD), lambda i, g, a, n: (i, kv_blk0 + n, v_col0 + g))],
        out_specs=pl.BlockSpec((None, tq, gw), lambda i, g, a, n: (i, a, g)),
        out_shape=jax.ShapeDtypeStruct((b, nq * tq, c), BF16),
        scratch_shapes=[pltpu.VMEM((group, tq, AT_HEAD), F32)] * 3,
        compiler_params=_params("parallel", "parallel", "parallel", "arbitrary"),
        name="flash",
    )(qn, kn, p)


def kernel(x, c, ctx, c_ctx, ada_down, ada_up, ada_bias, norm_g, ffn_w1, ffn_w3, ffn_w2, ev_w_in, ev_w_out, hy_short_w, hy_short_b, hy_pe_w1, hy_pe_b1, hy_pe_w2, hy_pe_b2, hy_sin_freq, hy_pe_w3, hy_bias, rw_mu, rw_w0, rw_w_up, rw_a0, rw_a_up, rw_g_up, rw_k_k, rw_k_a, rw_r_k, rw_ln_w, rw_ln_b, od_w_in, od_w_out, hg_lower_bounds, hg_norm_g, at_q_norm, at_k_norm):
    bsz, seq, d = x.shape
    ctx_len = ctx.shape[1]
    t = seq + ctx_len
    depth = ada_down.shape[0]
    half = d // 2
    tr = min(256, ctx_len)
    assert seq % ctx_len == 0 and ctx_len % LANES == 0 and bsz < SUBLANES

    xs = jnp.concatenate([x, ctx], axis=1)
    cond = jnp.concatenate([c, c_ctx[None], jnp.zeros((SUBLANES - bsz - 1, d), F32)], axis=0)
    m = _adaln(cond, ada_down, ada_up, ada_bias).reshape(depth, SUBLANES, N_MOD, d)
    mods = jnp.stack([m[:, :bsz], jnp.broadcast_to(m[:, bsz:bsz + 1], (depth, bsz, N_MOD, d))], axis=2)

    n1 = 2 * seq // LANES
    tables = _fft_tables(n1, LANES)
    cos, sin = _rope_tables(seq)
    lb_all = _hg_bounds(hg_lower_bounds)

    for l in range(depth):
        h = _norm_mod(xs, norm_g, mods, l, seq, tr, 0, 0, 1).reshape(bsz * t, d)
        if l % 2 == 0:
            e = l // 2
            w_in = ev_w_in[e]
            n_hy = 3 * half
            lw, la = rw_w_up.shape[2], rw_a_up.shape[2]
            w_lo = w_in[:, 2 * n_hy:]
            pad_c = lambda a: jnp.pad(a, ((0, 0), (0, LANES - a.shape[1])))
            w_lora = jnp.concatenate([pad_c(w_lo[:, :lw]), pad_c(w_lo[:, lw:lw + la]), w_lo[:, lw + la:]], axis=1)
            n_rw_heads = half // RW_HEAD
            w_rkv = w_in[:, n_hy:2 * n_hy].reshape(d, 3, n_rw_heads, RW_HEAD).swapaxes(2, 3).reshape(d, n_hy)
            w_main = jnp.concatenate([w_in[:, :n_hy], w_rkv], axis=1).astype(BF16)
            zh = _mm([h], w_main, 0, n_hy, F32).reshape(bsz, t, n_hy)
            zs = _mm([h], w_main, n_hy, n_hy, F32).reshape(bsz, t, n_hy)
            zl = _mm([h], w_lora.astype(BF16), 0, w_lora.shape[1], F32).reshape(bsz, t, -1)
            hy = (hy_short_w[e], hy_short_b[e], hy_pe_w1[e], hy_pe_b1[e], hy_pe_w2[e], hy_pe_b2[e],
                  hy_sin_freq[e], hy_pe_w3[e], hy_bias[e])
            rw = (rw_mu[e], rw_w0[e], rw_w_up[e], rw_a0[e], rw_a_up[e], rw_g_up[e], rw_k_k[e],
                  rw_k_a[e], rw_r_k[e], rw_ln_w[e], rw_ln_b[e])
            mix_a = _hyena(zh, hy, seq, ctx_len, tables).astype(BF16)
            mix_b = _rwkv(zs, zl, rw, seq, tr)
            w_rw_rows = ev_w_out[e][half:].reshape(n_rw_heads, RW_HEAD, d).swapaxes(0, 1).reshape(half, d)
            w_out = jnp.concatenate([ev_w_out[e][:half], w_rw_rows], axis=0)
        else:
            o = l // 2
            p = _mm([h], od_w_in[o].astype(BF16), 0, od_w_in.shape[2], F32).reshape(bsz, t, -1)
            lb = lb_all[l][None, :]
            o0 = _hg_scan(p, lb, half, seq, 0)
            o1 = _hg_scan(p, lb, half, seq, 1)
            mix_a = _hg_out(o0, o1, p, hg_norm_g[o][None], tr)
            qn, kn = _at_prep(p, half, at_q_norm[o][None], at_k_norm[o][None], cos, sin, seq, tr)
            tq = _pick(seq, (1024, 512, 256, 128))
            tk = _pick(t, (2816, 768, 512, 256, 128))
            at_l = _flash(qn, kn, p, half, 0, tq, seq // tq, 0, tk, t // tk)
            at_c = _flash(qn, kn, p, half, seq // ctx_len, ctx_len, 1, seq // ctx_len, ctx_len, 1)
            mix_b = jnp.concatenate([at_l, at_c], axis=1)
            w_out = od_w_out[o]
        y = _mm([mix_a.reshape(bsz * t, half), mix_b.reshape(bsz * t, half)], w_out.astype(BF16), 0, d, F32)
        xs = _resid(xs, y, norm_g, mods, l, seq, tr, 1, 2)
        h = _norm_mod(xs, norm_g, mods, l, seq, tr, 2, 3, 4).reshape(bsz * t, d)
        hid = _swiglu_up(h, ffn_w1[l].astype(BF16), ffn_w3[l].astype(BF16))
        y = _mm_ksplit(hid, ffn_w2[l].astype(BF16))
        xs = _resid(xs, y, norm_g, mods, l, seq, tr, 3, 5)
    return xs[:, :seq]
```
